```python
import math
import jax
import jax.numpy as jnp
from jax import lax
import numpy as np

D_MODEL = 2048
BATCH = 1
SEQ = 8192
DEPTH = 1

GLA_HEADS = 4
GLA_DK = 256
GLA_DV = 512
GLA_GATE_RANK = 16
GLA_GATE_TEMP = 16.0
GLA_CHUNK = 64
DSA_HEADS = 16
DSA_HEAD_DIM = 128
IDX_HEADS = 16
IDX_DIM = 64
DSA_TOPK = 256
DSA_QBLOCK = 128
MEM_LEN = 256
XATTN_HEADS = 4
XATTN_HEAD_DIM = D_MODEL // XATTN_HEADS
PEER_HEADS = 8
PEER_NKEYS = 128
PEER_NEXPERTS = PEER_NKEYS * PEER_NKEYS
PEER_KEY_DIM = 256
PEER_HALF = PEER_KEY_DIM // 2
PEER_TOPK = 16
PEER_BLOCK = 64
ROPE_THETA = 10000.0
RMS_EPS = 1e-6

IN_WIDTHS = (
    GLA_HEADS * GLA_DK,
    GLA_HEADS * GLA_DK,
    GLA_HEADS * GLA_DV,
    GLA_HEADS * GLA_DV,
    GLA_GATE_RANK,
    DSA_HEADS * DSA_HEAD_DIM,
    DSA_HEAD_DIM,
    DSA_HEAD_DIM,
    IDX_HEADS * IDX_DIM,
    IDX_DIM,
    IDX_HEADS,
    D_MODEL,
    D_MODEL,
)
IN_DIM = sum(IN_WIDTHS)

kernel_name = "hybrid_gla_dsa_peer_block"


def rms_norm(x, g):
    xf = x.astype(jnp.float32)
    y = xf * lax.rsqrt(jnp.mean(xf * xf, axis=-1, keepdims=True) + RMS_EPS)
    return (y * g.astype(jnp.float32)).astype(x.dtype)


def rope(x, pos):
    d = x.shape[-1]
    inv = ROPE_THETA ** (-jnp.arange(0, d, 2, dtype=jnp.float32) / d)
    ang = pos.astype(jnp.float32)[..., None] * inv
    cos = jnp.cos(ang)[:, :, None, :]
    sin = jnp.sin(ang)[:, :, None, :]
    xf = x.astype(jnp.float32)
    x1, x2 = xf[..., : d // 2], xf[..., d // 2:]
    out = jnp.concatenate([x1 * cos - x2 * sin, x2 * cos + x1 * sin], axis=-1)
    return out.astype(x.dtype)


def split_cols(p):
    points, acc = [], 0
    for w in IN_WIDTHS[:-1]:
        acc += w
        points.append(acc)
    return jnp.split(p, points, axis=-1)


def gla_chunked(q, k, v, log_a):
    B, S, H, dk = q.shape
    dv = v.shape[-1]
    C = GLA_CHUNK
    nc = S // C

    def to_chunks(t):
        return t.astype(jnp.float32).reshape(B, nc, C, H, -1).transpose(1, 0, 3, 2, 4)

    qc = to_chunks(q * (dk ** -0.5))
    kc, vc, gc = to_chunks(k), to_chunks(v), to_chunks(log_a)
    causal = jnp.tril(jnp.ones((C, C), dtype=bool))[None, None, :, :, None]

    def step(state, inp):
        qb, kb, vb, gb = inp
        b = jnp.cumsum(gb, axis=2)
        o_inter = jnp.einsum('bhcd,bhde->bhce', qb * jnp.exp(b), state)
        diff = jnp.where(causal, b[:, :, :, None, :] - b[:, :, None, :, :], -jnp.inf)
        attn = jnp.einsum('bhijd,bhjd->bhij', qb[:, :, :, None, :] * jnp.exp(diff), kb)
        o_intra = jnp.einsum('bhij,bhje->bhie', attn, vb)
        b_last = b[:, :, -1:, :]
        new_state = jnp.exp(b_last[:, :, 0, :])[..., None] * state + jnp.einsum(
            'bhjd,bhje->bhde', kb * jnp.exp(b_last - b), vb)
        return new_state, o_inter + o_intra

    s0 = jnp.zeros((B, H, dk, dv), jnp.float32)
    _, o = lax.scan(step, s0, (qc, kc, vc, gc))
    return o.transpose(1, 0, 3, 2, 4).reshape(B, S, H, dv).astype(q.dtype)


def dsa_attention(q, k, v, q_idx, k_idx, w_idx):
    B, S, H, dh = q.shape
    topk = min(DSA_TOPK, S // 4)
    QB = DSA_QBLOCK
    nblk = S // QB
    key_pos = jnp.arange(S)
    scale = dh ** -0.5

    def block(i):
        t0 = i * QB
        q_b = lax.dynamic_slice_in_dim(q, t0, QB, axis=1)
        qi_b = lax.dynamic_slice_in_dim(q_idx, t0, QB, axis=1)
        w_b = lax.dynamic_slice_in_dim(w_idx, t0, QB, axis=1)
        t_pos = t0 + jnp.arange(QB)
        s_idx = jnp.einsum('bqhd,bsd->bqhs', qi_b, k_idx).astype(jnp.float32) * (IDX_DIM ** -0.5)
        score = jnp.einsum('bqhs,bqh->bqs', jax.nn.relu(s_idx),
                           w_b.astype(jnp.float32)) * (IDX_HEADS ** -0.5)
        visible = key_pos[None, :] <= t_pos[:, None]
        score = jnp.where(visible[None], score, -jnp.inf)
        _, sel = lax.top_k(score, topk)
        k_sel = jax.vmap(lambda kk, ii: kk[ii])(k, sel)
        v_sel = jax.vmap(lambda vv, ii: vv[ii])(v, sel)
        valid = sel <= t_pos[None, :, None]
        logits = jnp.einsum('bqhd,bqkd->bhqk', q_b, k_sel).astype(jnp.float32) * scale
        logits = jnp.where(valid[:, None], logits, -jnp.inf)
        p = jax.nn.softmax(logits, axis=-1).astype(v.dtype)
        return jnp.einsum('bhqk,bqkd->bqhd', p, v_sel)

    out = lax.map(block, jnp.arange(nblk))
    return out.transpose(1, 0, 2, 3, 4).reshape(B, S, H, dh)


def memory_cross_attention(hn, memn, wq, wk, wv, wo):
    B, S, _ = hn.shape
    M = memn.shape[1]
    q = (hn @ wq).reshape(B, S, XATTN_HEADS, XATTN_HEAD_DIM)
    k = (memn @ wk).reshape(B, M, XATTN_HEADS, XATTN_HEAD_DIM)
    v = (memn @ wv).reshape(B, M, XATTN_HEADS, XATTN_HEAD_DIM)
    logits = jnp.einsum('bshd,bmhd->bhsm', q, k).astype(jnp.float32) * (XATTN_HEAD_DIM ** -0.5)
    p = jax.nn.softmax(logits, axis=-1).astype(v.dtype)
    o = jnp.einsum('bhsm,bmhd->bshd', p, v).reshape(B, S, XATTN_HEADS * XATTN_HEAD_DIM)
    return o @ wo


def peer_ffn(xn, wq, sub_keys, u_tab, v_tab):
    B, S, D = xn.shape
    q = (xn @ wq).reshape(B, S, PEER_HEADS, 2, PEER_HALF)
    s = jnp.einsum('bshpd,hpnd->bshpn', q, sub_keys).astype(jnp.float32)
    top_v, top_i = lax.top_k(s, PEER_TOPK)
    cand = (top_v[..., 0, :, None] + top_v[..., 1, None, :]).reshape(B, S, PEER_HEADS, PEER_TOPK * PEER_TOPK)
    cand_v, cand_i = lax.top_k(cand, PEER_TOPK)
    i1 = jnp.take_along_axis(top_i[..., 0, :], cand_i // PEER_TOPK, axis=-1)
    i2 = jnp.take_along_axis(top_i[..., 1, :], cand_i % PEER_TOPK, axis=-1)
    experts = i1 * PEER_NKEYS + i2
    gates = jax.nn.softmax(cand_v, axis=-1).astype(xn.dtype)
    T = B * S
    nb = T // PEER_BLOCK
    x_b = xn.reshape(nb, PEER_BLOCK, D)
    e_b = experts.reshape(nb, PEER_BLOCK, PEER_HEADS * PEER_TOPK)
    g_b = gates.reshape(nb, PEER_BLOCK, PEER_HEADS * PEER_TOPK)

    def block(args):
        xb, eb, gb = args
        act = jax.nn.gelu(jnp.einsum('td,tkd->tk', xb, u_tab[eb]))
        return jnp.einsum('tk,tkd->td', gb * act, v_tab[eb])

    out = lax.map(block, (x_b, e_b, g_b))
    return out.reshape(B, S, D)


def setup_inputs(seed: int = 0) -> dict:
    key = jax.random.key(seed)
    ks = jax.random.split(key, 24)
    f32 = jnp.float32
    D = D_MODEL
    L = DEPTH

    def nrm(k, shape, scale):
        return jax.random.normal(k, shape, f32) * scale

    def gain(k, shape):
        return 1.0 + 0.02 * jax.random.normal(k, shape, f32)

    return {
        "x": nrm(ks[0], (BATCH, SEQ, D), 1.0),
        "mem": nrm(ks[1], (BATCH, MEM_LEN, D), 1.0),
        "positions": jnp.broadcast_to(jnp.arange(SEQ, dtype=jnp.int32), (BATCH, SEQ)),
        "norm_mix_g": gain(ks[2], (L, D)),
        "w_in": nrm(ks[3], (L, D, IN_DIM), D ** -0.5),
        "gla_gate_w2": nrm(ks[4], (L, GLA_GATE_RANK, GLA_HEADS * GLA_DK), GLA_GATE_RANK ** -0.5),
        "gla_gate_b": nrm(ks[5], (L, GLA_HEADS * GLA_DK), 0.1),
        "gla_head_norm_g": gain(ks[6], (L, GLA_HEADS, GLA_DV)),
        "w_proj_gla": nrm(ks[7], (L, GLA_HEADS * GLA_DV, D), (GLA_HEADS * GLA_DV) ** -0.5),
        "w_proj_dsa": nrm(ks[8], (L, DSA_HEADS * DSA_HEAD_DIM, D), (DSA_HEADS * DSA_HEAD_DIM) ** -0.5),
        "w_out": nrm(ks[9], (L, D, D), D ** -0.5),
        "norm_x_g": gain(ks[10], (L, D)),
        "norm_mem_g": gain(ks[11], (L, D)),
        "w_xq": nrm(ks[12], (L, D, XATTN_HEADS * XATTN_HEAD_DIM), D ** -0.5),
        "w_xk": nrm(ks[13], (L, D, XATTN_HEADS * XATTN_HEAD_DIM), D ** -0.5),
        "w_xv": nrm(ks[14], (L, D, XATTN_HEADS * XATTN_HEAD_DIM), D ** -0.5),
        "w_xo": nrm(ks[15], (L, XATTN_HEADS * XATTN_HEAD_DIM, D), (XATTN_HEADS * XATTN_HEAD_DIM) ** -0.5),
        "norm_ffn_g": gain(ks[16], (L, D)),
        "peer_wq": nrm(ks[17], (L, D, PEER_HEADS * PEER_KEY_DIM), D ** -0.5),
        "peer_sub_keys": nrm(ks[18], (L, PEER_HEADS, 2, PEER_NKEYS, PEER_HALF), PEER_HALF ** -0.5),
        "peer_u": nrm(ks[19], (L, PEER_NEXPERTS, D), D ** -0.5),
        "peer_v": nrm(ks[20], (L, PEER_NEXPERTS, D), 0.5),
        "norm_final_g": gain(ks[21], (D,)),
    }


def reference(x, mem, positions, norm_mix_g, w_in, gla_gate_w2, gla_gate_b, gla_head_norm_g,
              w_proj_gla, w_proj_dsa, w_out, norm_x_g, norm_mem_g, w_xq, w_xk, w_xv, w_xo,
              norm_ffn_g, peer_wq, peer_sub_keys, peer_u, peer_v, norm_final_g):
    B, S, D = x.shape
    h = x
    for l in range(DEPTH):
        a = rms_norm(h, norm_mix_g[l])
        (g_q, g_k, g_v, g_r, g_low, d_q, d_k, d_v, i_q, i_k, i_w, gate_a, gate_b) = split_cols(a @ w_in[l])

        log_a = jax.nn.log_sigmoid((g_low @ gla_gate_w2[l] + gla_gate_b[l]).astype(jnp.float32)) / GLA_GATE_TEMP
        o_a = gla_chunked(g_q.reshape(B, S, GLA_HEADS, GLA_DK),
                          g_k.reshape(B, S, GLA_HEADS, GLA_DK),
                          g_v.reshape(B, S, GLA_HEADS, GLA_DV),
                          log_a.reshape(B, S, GLA_HEADS, GLA_DK))
        o_a = rms_norm(o_a, gla_head_norm_g[l]).reshape(B, S, GLA_HEADS * GLA_DV) * jax.nn.silu(g_r)
        y_a = o_a @ w_proj_gla[l]

        q_b = rope(d_q.reshape(B, S, DSA_HEADS, DSA_HEAD_DIM), positions)
        k_b = rope(d_k[:, :, None, :], positions)[:, :, 0, :]
        qi = rope(i_q.reshape(B, S, IDX_HEADS, IDX_DIM), positions)
        ki = rope(i_k[:, :, None, :], positions)[:, :, 0, :]
        o_b = dsa_attention(q_b, k_b, d_v, qi, ki, i_w).reshape(B, S, DSA_HEADS * DSA_HEAD_DIM)
        y_b = o_b @ w_proj_dsa[l]

        merged = jax.nn.sigmoid(gate_a) * y_a + jax.nn.sigmoid(gate_b) * y_b
        h = h + merged @ w_out[l]

        h = h + memory_cross_attention(rms_norm(h, norm_x_g[l]), rms_norm(mem, norm_mem_g[l]),
                                       w_xq[l], w_xk[l], w_xv[l], w_xo[l])

        h = h + peer_ffn(rms_norm(h, norm_ffn_g[l]), peer_wq[l], peer_sub_keys[l], peer_u[l], peer_v[l])
    return rms_norm(h, norm_final_g)
```

```python
import functools
import math

import numpy as np
import jax
import jax.numpy as jnp
from jax import lax
from jax.experimental import pallas as pl
from jax.experimental.pallas import tpu as pltpu

F32 = jnp.float32
BF16 = jnp.bfloat16

D_MODEL = 2048
GLA_HEADS = 4
GLA_DK = 256
GLA_DV = 512
GLA_GATE_RANK = 16
GLA_GATE_TEMP = 16.0
DSA_HEADS = 16
DSA_HEAD_DIM = 128
IDX_HEADS = 16
IDX_DIM = 64
DSA_TOPK = 256
XATTN_HEADS = 4
XATTN_HEAD_DIM = D_MODEL // XATTN_HEADS
PEER_HEADS = 8
PEER_NKEYS = 128
PEER_HALF = 128
PEER_TOPK = 16
ROPE_THETA = 10000.0
RMS_EPS = 1e-6

LANES = 128
V7X_VMEM_LIMIT_BYTES = 56 * 1024 * 1024

COL_GV = 0
COL_GR = 2048
COL_DQ = 4096
COL_GA = 6144
COL_GB = 8192
COL_GQ = 10240
COL_GK = 11264
COL_IQ = 12288
COL_DK = 13312
COL_DV = 13440
COL_TAIL = 13568
TAIL_GLOW = 64
TAIL_IW = 80
NP_COLS = 13824

_SRC = dict(gq=0, gk=1024, gv=2048, gr=4096, glow=6144, dq=6160, dk=8208, dv=8336,
            iq=8464, ik=9488, iw=9552, ga=9568, gb=11616, end=13664)


def _cparams(sem, vmem=V7X_VMEM_LIMIT_BYTES):
    return pltpu.CompilerParams(dimension_semantics=sem, vmem_limit_bytes=vmem)


def _dot_nt(a, b):
    return lax.dot_general(a, b, (((1,), (1,)), ((), ())), preferred_element_type=F32)


def _split3(x):
    hi = x.astype(BF16)
    r1 = x - hi.astype(F32)
    mid = r1.astype(BF16)
    lo = (r1 - mid.astype(F32)).astype(BF16)
    return hi, mid, lo


def _norm_mm_kernel(x_ref, g_ref, w_ref, o_ref, xn_ref, a_scr):
    @pl.when(pl.program_id(1) == 0)
    def _():
        x = x_ref[...]
        y = x * lax.rsqrt(jnp.mean(x * x, axis=-1, keepdims=True) + RMS_EPS) * g_ref[...]
        a_scr[...] = y.astype(BF16)
        xn_ref[...] = a_scr[...]

    o_ref[...] = jnp.dot(a_scr[...], w_ref[...], preferred_element_type=F32).astype(o_ref.dtype)


def _norm_mm(x, g, w, out_dtype, tm, tn, name):
    M, K = x.shape
    N = w.shape[1]
    tm, tn = min(tm, M), min(tn, N)
    return pl.pallas_call(
        _norm_mm_kernel,
        out_shape=(jax.ShapeDtypeStruct((M, N), out_dtype), jax.ShapeDtypeStruct((M, K), BF16)),
        grid=(M // tm, N // tn),
        in_specs=[pl.BlockSpec((tm, K), lambda i, j: (i, 0)),
                  pl.BlockSpec((1, K), lambda i, j: (0, 0)),
                  pl.BlockSpec((K, tn), lambda i, j: (0, j))],
        out_specs=(pl.BlockSpec((tm, tn), lambda i, j: (i, j)),
                   pl.BlockSpec((tm, K), lambda i, j: (i, 0))),
        scratch_shapes=[pltpu.VMEM((tm, K), BF16)],
        compiler_params=_cparams(("arbitrary", "arbitrary")),
        name=name,
    )(x, g.reshape(1, K), w)


def _mm_res_kernel(a_ref, w_ref, r_ref, o_ref):
    o_ref[...] = r_ref[...] + jnp.dot(a_ref[...], w_ref[...], preferred_element_type=F32)


def _mm_res(a, w, res, tm, tn, name):
    M, K = a.shape
    N = w.shape[1]
    tm, tn = min(tm, M), min(tn, N)
    return pl.pallas_call(
        _mm_res_kernel,
        out_shape=jax.ShapeDtypeStruct((M, N), F32),
        grid=(M // tm, N // tn),
        in_specs=[pl.BlockSpec((tm, K), lambda i, j: (i, 0)),
                  pl.BlockSpec((K, tn), lambda i, j: (0, j)),
                  pl.BlockSpec((tm, tn), lambda i, j: (i, j))],
        out_specs=pl.BlockSpec((tm, tn), lambda i, j: (i, j)),
        compiler_params=_cparams(("arbitrary", "arbitrary")),
        name=name,
    )(a, w, res)


def _merge_kernel(oa_ref, ob_ref, wa_ref, wb_ref, ga_ref, gb_ref, o_ref):
    ya = jnp.dot(oa_ref[...], wa_ref[...], preferred_element_type=F32)
    yb = jnp.dot(ob_ref[...], wb_ref[...], preferred_element_type=F32)
    o_ref[...] = (jax.nn.sigmoid(ga_ref[...]) * ya + jax.nn.sigmoid(gb_ref[...]) * yb).astype(o_ref.dtype)


def _merge(oa, ob, wa, wb, proj, tm, tn):
    M, K = oa.shape
    N = wa.shape[1]
    tm, tn = min(tm, M), min(tn, N)
    ca, cb = COL_GA // tn, COL_GB // tn
    return pl.pallas_call(
        _merge_kernel,
        out_shape=jax.ShapeDtypeStruct((M, N), BF16),
        grid=(M // tm, N // tn),
        in_specs=[pl.BlockSpec((tm, K), lambda i, j: (i, 0)),
                  pl.BlockSpec((tm, K), lambda i, j: (i, 0)),
                  pl.BlockSpec((K, tn), lambda i, j: (0, j)),
                  pl.BlockSpec((K, tn), lambda i, j: (0, j)),
                  pl.BlockSpec((tm, tn), lambda i, j: (i, ca + j)),
                  pl.BlockSpec((tm, tn), lambda i, j: (i, cb + j))],
        out_specs=pl.BlockSpec((tm, tn), lambda i, j: (i, j)),
        compiler_params=_cparams(("arbitrary", "arbitrary")),
        name="merge",
    )(oa, ob, wa, wb, proj, proj)


GLA_C = 128
GLA_LEVELS = (64, 32, 16)
GLA_DIAG = 16


def _gla_matrices(C):
    t = np.arange(C)[:, None]
    u = np.arange(C)[None, :]
    L = (u <= t).astype(np.float32)
    mats = [L, (u > t).astype(np.float32)]
    for m in GLA_LEVELS:
        anchor = (t // (2 * m)) * (2 * m) + m - 1
        mats.append(L - (u <= anchor).astype(np.float32))
    anchor = (t // GLA_DIAG) * GLA_DIAG
    mats.append(L - (u <= anchor).astype(np.float32))
    return np.concatenate(mats, axis=0)


def _gla_kernel(q_ref, k_ref, v_ref, r_ref, tail_ref, w2_ref, gb_ref, hg_ref, mats_ref, o_ref, st_scr):
    C = GLA_C

    @pl.when(pl.program_id(1) == 0)
    def _():
        st_scr[...] = jnp.zeros_like(st_scr)

    t_hi, t_mid, t_lo = _split3(tail_ref[...])
    w_hi, w_mid, w_lo = _split3(w2_ref[...])
    z = (jnp.dot(t_hi, w_hi, preferred_element_type=F32) + jnp.dot(t_hi, w_mid, preferred_element_type=F32)
         + jnp.dot(t_mid, w_hi, preferred_element_type=F32) + jnp.dot(t_lo, w_hi, preferred_element_type=F32)
         + jnp.dot(t_hi, w_lo, preferred_element_type=F32) + jnp.dot(t_mid, w_mid, preferred_element_type=F32))
    z = z + gb_ref[...]
    log_a = (jnp.minimum(z, 0.0) - jnp.log1p(jnp.exp(-jnp.abs(z)))) * (1.0 / GLA_GATE_TEMP)

    a_hi, a_mid, a_lo = _split3(log_a)
    mats = mats_ref[...]
    dec = (jnp.dot(mats, a_hi, preferred_element_type=F32) + jnp.dot(mats, a_mid, preferred_element_type=F32)
           + jnp.dot(mats, a_lo, preferred_element_type=F32))
    b = dec[0:C]
    b_rest = dec[C:2 * C]

    q = q_ref[...] * (GLA_DK ** -0.5)
    k = k_ref[...]
    v = v_ref[...]
    v_bf = v.astype(BF16)

    row = lax.broadcasted_iota(jnp.int32, (C, C), 0)
    col = lax.broadcasted_iota(jnp.int32, (C, C), 1)
    attn = jnp.zeros((C, C), F32)
    for li, m in enumerate(GLA_LEVELS):
        d = dec[(2 + li) * C:(3 + li) * C]
        qs = (q * jnp.exp(jnp.minimum(d, 0.0))).astype(BF16)
        ks = (k * jnp.exp(jnp.minimum(-d, 0.0))).astype(BF16)
        sh = int(math.log2(2 * m))
        mask = ((row >> sh) == (col >> sh)) & ((row & m) != 0) & ((col & m) == 0)
        attn = attn + jnp.where(mask, _dot_nt(qs, ks), 0.0)
    d = dec[(2 + len(GLA_LEVELS)) * C:(3 + len(GLA_LEVELS)) * C]
    qs = (q * jnp.exp(d)).astype(BF16)
    ks = (k * jnp.exp(-d)).astype(BF16)
    sh = int(math.log2(GLA_DIAG))
    mask = ((row >> sh) == (col >> sh)) & (col <= row)
    attn = attn + jnp.where(mask, _dot_nt(qs, ks), 0.0)

    st = st_scr[...]
    o = _dot_nt((q * jnp.exp(b)).astype(BF16), st.astype(BF16))
    o = o + jnp.dot(attn.astype(BF16), v_bf, preferred_element_type=F32)

    kd = (k * jnp.exp(b_rest)).astype(BF16)
    upd = jnp.dot(v.T.astype(BF16), kd, preferred_element_type=F32)
    st_scr[...] = st * jnp.exp(b[C - 1:C, :]) + upd

    y = o * lax.rsqrt(jnp.mean(o * o, axis=-1, keepdims=True) + RMS_EPS) * hg_ref[...]
    r = r_ref[...]
    o_ref[...] = (y * (r * jax.nn.sigmoid(r))).astype(o_ref.dtype)


def _gla(proj, w2p, gate_b, head_g):
    S = proj.shape[0]
    C = GLA_C
    mats = jnp.asarray(_gla_matrices(C), dtype=BF16)
    nm = mats.shape[0]
    cq, ck = COL_GQ // GLA_DK, COL_GK // GLA_DK
    cv, cr = COL_GV // GLA_DV, COL_GR // GLA_DV
    ct = COL_TAIL // LANES
    return pl.pallas_call(
        _gla_kernel,
        out_shape=jax.ShapeDtypeStruct((S, GLA_HEADS * GLA_DV), BF16),
        grid=(GLA_HEADS, S // C),
        in_specs=[pl.BlockSpec((C, GLA_DK), lambda h, c: (c, cq + h)),
                  pl.BlockSpec((C, GLA_DK), lambda h, c: (c, ck + h)),
                  pl.BlockSpec((C, GLA_DV), lambda h, c: (c, cv + h)),
                  pl.BlockSpec((C, GLA_DV), lambda h, c: (c, cr + h)),
                  pl.BlockSpec((C, LANES), lambda h, c: (c, ct)),
                  pl.BlockSpec((LANES, GLA_DK), lambda h, c: (0, h)),
                  pl.BlockSpec((1, GLA_DK), lambda h, c: (0, h)),
                  pl.BlockSpec((1, GLA_DV), lambda h, c: (0, h)),
                  pl.BlockSpec((nm, C), lambda h, c: (0, 0))],
        out_specs=pl.BlockSpec((C, GLA_DV), lambda h, c: (c, h)),
        scratch_shapes=[pltpu.VMEM((GLA_DV, GLA_DK), F32)],
        compiler_params=_cparams(("arbitrary", "arbitrary")),
        name="gla",
    )(proj, proj, proj, proj, proj, w2p, gate_b.reshape(1, -1), head_g.reshape(1, -1), mats)


def _rope_kernel(pos_ref, dq_ref, dkv_ref, iq_ref, tail_ref, inv_ref, sgn_ref,
                 qb_ref, kb_ref, vb_ref, qi_ref, kia_ref, kib_ref):
    pos = pos_ref[...].astype(F32)
    lane = lax.broadcasted_iota(jnp.int32, (pos.shape[0], LANES), 1)

    ang = pos * inv_ref[0:1, :]
    c128 = jnp.cos(ang)
    s128 = jnp.sin(ang) * sgn_ref[0:1, :]
    ang = pos * inv_ref[1:2, :]
    c64 = jnp.cos(ang)
    s64 = jnp.sin(ang) * sgn_ref[1:2, :]

    def rot128(x):
        return x * c128 + pltpu.roll(x, 64, 1) * s128

    def rot64(x):
        swapped = jnp.where(lane % 64 < 32, pltpu.roll(x, 96, 1), pltpu.roll(x, 32, 1))
        return x * c64 + swapped * s64

    scale = DSA_HEAD_DIM ** -0.5
    for h in range(DSA_HEADS):
        sl = slice(h * LANES, (h + 1) * LANES)
        qb_ref[:, sl] = (rot128(dq_ref[:, sl]) * scale).astype(BF16)
    kb_ref[...] = rot128(dkv_ref[:, 0:LANES]).astype(BF16)
    vb_ref[...] = dkv_ref[:, LANES:2 * LANES].astype(BF16)
    for h in range(IDX_HEADS // 2):
        sl = slice(h * LANES, (h + 1) * LANES)
        qi_ref[:, sl] = rot64(iq_ref[:, sl]).astype(BF16)
    ki = jnp.where(lane < IDX_DIM, rot64(tail_ref[...]), 0.0)
    kia_ref[...] = ki.astype(BF16)
    kib_ref[...] = pltpu.roll(ki, 64, 1).astype(BF16)


def _rope(proj, positions, tm):
    S = proj.shape[0]
    tm = min(tm, S)
    inv128 = ROPE_THETA ** (-jnp.arange(0, DSA_HEAD_DIM, 2, dtype=F32) / DSA_HEAD_DIM)
    inv64 = ROPE_THETA ** (-jnp.arange(0, IDX_DIM, 2, dtype=F32) / IDX_DIM)
    inv = jnp.stack([jnp.tile(inv128, 2), jnp.tile(inv64, 4)])
    sgn = jnp.asarray(np.stack([np.repeat([-1.0, 1.0], 64), np.tile(np.repeat([-1.0, 1.0], 32), 2)]), F32)
    nq = DSA_HEADS * DSA_HEAD_DIM
    ni = IDX_HEADS * IDX_DIM
    outs = (jax.ShapeDtypeStruct((S, nq), BF16), jax.ShapeDtypeStruct((S, LANES), BF16),
            jax.ShapeDtypeStruct((S, LANES), BF16), jax.ShapeDtypeStruct((S, ni), BF16),
            jax.ShapeDtypeStruct((S, LANES), BF16), jax.ShapeDtypeStruct((S, LANES), BF16))
    row = lambda w: pl.BlockSpec((tm, w), lambda i: (i, 0))
    return pl.pallas_call(
        _rope_kernel,
        out_shape=outs,
        grid=(S // tm,),
        in_specs=[pl.BlockSpec((tm, 1), lambda i: (i, 0)),
                  pl.BlockSpec((tm, nq), lambda i: (i, COL_DQ // nq)),
                  pl.BlockSpec((tm, 2 * LANES), lambda i: (i, COL_DK // (2 * LANES))),
                  pl.BlockSpec((tm, ni), lambda i: (i, COL_IQ // ni)),
                  pl.BlockSpec((tm, LANES), lambda i: (i, COL_TAIL // LANES)),
                  pl.BlockSpec((2, LANES), lambda i: (0, 0)),
                  pl.BlockSpec((2, LANES), lambda i: (0, 0))],
        out_specs=(row(nq), row(LANES), row(LANES), row(ni), row(LANES), row(LANES)),
        compiler_params=_cparams(("arbitrary",)),
        name="rope",
    )(positions.reshape(S, 1), proj, proj, proj, proj, inv, sgn)


DSA_TQ = 128
DSA_TK = 256
_KEY_ALL = -2139095040


def _sort_key(x):
    bits = lax.bitcast_convert_type(x, jnp.int32)
    return bits ^ ((bits >> 31) & jnp.int32(0x7FFFFFFF))


def _key_to_f32(key):
    return lax.bitcast_convert_type(key ^ ((key >> 31) & jnp.int32(0x7FFFFFFF)), F32)


def _dsa_kernel(topk, qi_ref, qb_ref, tail_ref, kia_ref, kib_ref, kb_ref, vb_ref, o_ref,
                key_scr, m_scr, l_scr, acc_scr):
    tq, tk = DSA_TQ, DSA_TK
    i = pl.program_id(0)
    t0 = i * tq
    nvis = (t0 + tq - 1) // tk + 1

    w_idx = tail_ref[:, TAIL_IW:TAIL_IW + IDX_HEADS] * ((IDX_DIM ** -0.5) * (IDX_HEADS ** -0.5))
    t_row = t0 + lax.broadcasted_iota(jnp.int32, (tq, tk), 0)
    lane = lax.broadcasted_iota(jnp.int32, (tq, tk), 1)

    def score_body(j, carry):
        smin, smax = carry
        off = pl.multiple_of(j * tk, tk)
        ka = kia_ref[pl.ds(off, tk), :]
        kb_ = kib_ref[pl.ds(off, tk), :]
        acc = jnp.zeros((tq, tk), F32)
        for hp in range(IDX_HEADS // 2):
            qp = qi_ref[:, hp * LANES:(hp + 1) * LANES]
            acc = acc + jnp.maximum(_dot_nt(qp, ka), 0.0) * w_idx[:, 2 * hp:2 * hp + 1]
            acc = acc + jnp.maximum(_dot_nt(qp, kb_), 0.0) * w_idx[:, 2 * hp + 1:2 * hp + 2]
        vis = (off + lane) <= t_row
        key_scr[j] = _sort_key(jnp.where(vis, acc, -jnp.inf))
        smin = jnp.minimum(smin, jnp.min(jnp.where(vis, acc, jnp.inf), axis=1, keepdims=True))
        smax = jnp.maximum(smax, jnp.max(jnp.where(vis, acc, -jnp.inf), axis=1, keepdims=True))
        return smin, smax

    smin, smax = lax.fori_loop(
        0, nvis, score_body, (jnp.full((tq, 1), jnp.inf, F32), jnp.full((tq, 1), -jnp.inf, F32)))

    n_vis = (t0 + lax.broadcasted_iota(jnp.int32, (tq, 1), 0) + 1).astype(F32)
    select_all = n_vis <= topk
    lo0 = jnp.where(select_all, jnp.int32(_KEY_ALL), _sort_key(smin))
    hi0 = jnp.where(select_all, jnp.int32(_KEY_ALL), _sort_key(smax))

    def count_ge(mid):
        def body(j, c):
            return c + jnp.where(key_scr[j] >= mid, 1.0, 0.0)
        c = lax.fori_loop(0, nvis, body, jnp.zeros((tq, tk), F32))
        return jnp.sum(c, axis=1, keepdims=True)

    def bis_cond(st):
        lo, hi, c_lo = st
        active = (lo < hi) & (c_lo != topk)
        return jnp.max(jnp.where(active, 1.0, 0.0)) > 0.5

    def bis_body(st):
        lo, hi, c_lo = st
        active = (lo < hi) & (c_lo != topk)
        mid_f = 0.5 * _key_to_f32(lo) + 0.5 * _key_to_f32(hi)
        mid = jnp.clip(_sort_key(mid_f), lo + 1, hi)
        mid = jnp.where(active, mid, lo)
        c = count_ge(mid)
        ge = c >= topk
        lo_n = jnp.where(active & ge, mid, lo)
        c_n = jnp.where(active & ge, c, c_lo)
        hi_n = jnp.where(active & jnp.logical_not(ge), mid - 1, hi)
        return lo_n, hi_n, c_n

    thr, _, _ = lax.while_loop(bis_cond, bis_body, (lo0, hi0, n_vis))

    m_scr[...] = jnp.full(m_scr.shape, -1e30, F32)
    l_scr[...] = jnp.zeros(l_scr.shape, F32)
    acc_scr[...] = jnp.zeros(acc_scr.shape, F32)

    def attn_body(j, carry):
        off = pl.multiple_of(j * tk, tk)
        kblk = kb_ref[pl.ds(off, tk), :]
        vblk = vb_ref[pl.ds(off, tk), :]
        bias = jnp.where(key_scr[j] >= thr, 0.0, -jnp.inf)
        for h in range(DSA_HEADS):
            s = _dot_nt(qb_ref[:, h * LANES:(h + 1) * LANES], kblk) + bias
            m_old = m_scr[h]
            m_new = jnp.maximum(m_old, jnp.max(s, axis=1, keepdims=True))
            p = jnp.exp(s - m_new)
            alpha = jnp.exp(m_old - m_new)
            l_scr[h] = alpha * l_scr[h] + jnp.sum(p, axis=1, keepdims=True)
            acc_scr[h] = alpha * acc_scr[h] + jnp.dot(p.astype(BF16), vblk, preferred_element_type=F32)
            m_scr[h] = m_new
        return carry

    lax.fori_loop(0, nvis, attn_body, 0)
    for h in range(DSA_HEADS):
        o_ref[:, h * LANES:(h + 1) * LANES] = (acc_scr[h] / l_scr[h]).astype(o_ref.dtype)


def _dsa(qi, qb, proj, kia, kib, kb, vb):
    S = qb.shape[0]
    tq, tk = DSA_TQ, DSA_TK
    topk = min(DSA_TOPK, S // 4)
    nq = DSA_HEADS * DSA_HEAD_DIM
    full = lambda: pl.BlockSpec((S, LANES), lambda i: (0, 0))
    return pl.pallas_call(
        functools.partial(_dsa_kernel, topk),
        out_shape=jax.ShapeDtypeStruct((S, nq), BF16),
        grid=(S // tq,),
        in_specs=[pl.BlockSpec((tq, IDX_HEADS * IDX_DIM), lambda i: (i, 0)),
                  pl.BlockSpec((tq, nq), lambda i: (i, 0)),
                  pl.BlockSpec((tq, LANES), lambda i: (i, COL_TAIL // LANES)),
                  full(), full(), full(), full()],
        out_specs=pl.BlockSpec((tq, nq), lambda i: (i, 0)),
        scratch_shapes=[pltpu.VMEM((S // tk, tq, tk), jnp.int32),
                        pltpu.VMEM((DSA_HEADS, tq, 1), F32),
                        pltpu.VMEM((DSA_HEADS, tq, 1), F32),
                        pltpu.VMEM((DSA_HEADS, tq, DSA_HEAD_DIM), F32)],
        compiler_params=_cparams(("arbitrary",)),
        name="dsa",
    )(qi, qb, proj, kia, kib, kb, vb)


def _xattn_kernel(q_ref, kv_ref, o_ref):
    d = XATTN_HEAD_DIM
    nkv = XATTN_HEADS * d
    for h in range(XATTN_HEADS):
        q = q_ref[:, h * d:(h + 1) * d]
        k = kv_ref[:, h * d:(h + 1) * d]
        v = kv_ref[:, nkv + h * d:nkv + (h + 1) * d]
        s = _dot_nt(q, k) * (d ** -0.5)
        s = s - jnp.max(s, axis=1, keepdims=True)
        p = jnp.exp(s)
        p = p / jnp.sum(p, axis=1, keepdims=True)
        o_ref[:, h * d:(h + 1) * d] = jnp.dot(p.astype(BF16), v, preferred_element_type=F32).astype(o_ref.dtype)


def _xattn(q, kv, tm):
    S, N = q.shape
    M = kv.shape[0]
    tm = min(tm, S)
    return pl.pallas_call(
        _xattn_kernel,
        out_shape=jax.ShapeDtypeStruct((S, N), BF16),
        grid=(S // tm,),
        in_specs=[pl.BlockSpec((tm, N), lambda i: (i, 0)),
                  pl.BlockSpec((M, 2 * N), lambda i: (0, 0))],
        out_specs=pl.BlockSpec((tm, N), lambda i: (i, 0)),
        compiler_params=_cparams(("arbitrary",)),
        name="xattn",
    )(q, kv)


PEER_TS = 256


def _top16_distinct(s):
    vals, cnts = [], []
    rem = s
    for _ in range(PEER_TOPK):
        m = jnp.max(rem, axis=0, keepdims=True)
        eq = rem == m
        cnts.append(jnp.sum(jnp.where(eq, 1.0, 0.0), axis=0, keepdims=True))
        vals.append(m)
        rem = jnp.where(eq, -jnp.inf, rem)
    return vals, cnts


def _stack_rows(rows):
    n, T = len(rows), rows[0].shape[1]
    idx = lax.broadcasted_iota(jnp.int32, (n, T), 0)
    out = jnp.broadcast_to(rows[0], (n, T))
    for r in range(1, n):
        out = jnp.where(idx == r, rows[r], out)
    return out


def _peer_select_kernel(q_ref, sk_ref, s1_ref, s2_ref, e1_ref, e2_ref, thr_ref):
    T = q_ref.shape[0]
    for h in range(PEER_HEADS):
        s1 = _dot_nt(sk_ref[h, 0], q_ref[:, (2 * h) * PEER_HALF:(2 * h + 1) * PEER_HALF])
        s2 = _dot_nt(sk_ref[h, 1], q_ref[:, (2 * h + 1) * PEER_HALF:(2 * h + 2) * PEER_HALF])
        a_vals, a_cnts = _top16_distinct(s1)
        b_vals, b_cnts = _top16_distinct(s2)
        b_mat = _stack_rows(b_vals)
        bc_mat = _stack_rows(b_cnts)
        cand = jnp.concatenate([a + b_mat for a in a_vals], axis=0)
        wgt = jnp.concatenate([c * bc_mat for c in a_cnts], axis=0)
        rem = cand
        n = jnp.zeros((1, T), F32)
        thr = jnp.full((1, T), -jnp.inf, F32)
        for _ in range(PEER_TOPK):
            m = jnp.max(rem, axis=0, keepdims=True)
            eq = rem == m
            thr = jnp.where(n < PEER_TOPK, m, thr)
            n = n + jnp.sum(jnp.where(eq, wgt, 0.0), axis=0, keepdims=True)
            rem = jnp.where(eq, -jnp.inf, rem)
        vmax = a_vals[0] + b_vals[0]
        z = jnp.sum(jnp.where(cand >= thr, wgt * jnp.exp(cand - vmax), 0.0), axis=0, keepdims=True)
        s1_ref[h] = s1
        s2_ref[h] = s2
        e1_ref[h] = jnp.exp(s1 - a_vals[0])
        e2_ref[h] = jnp.exp(s2 - b_vals[0]) / z
        thr_ref[h] = jnp.broadcast_to(thr, (8, T))


def _peer_select(q, sub_keys):
    S = q.shape[0]
    ts = min(PEER_TS, S)
    big = jax.ShapeDtypeStruct((PEER_HEADS, PEER_NKEYS, S), F32)
    bspec = pl.BlockSpec((PEER_HEADS, PEER_NKEYS, ts), lambda i: (0, 0, i))
    return pl.pallas_call(
        _peer_select_kernel,
        out_shape=(big, big, big, big, jax.ShapeDtypeStruct((PEER_HEADS, 8, S), F32)),
        grid=(S // ts,),
        in_specs=[pl.BlockSpec((ts, q.shape[1]), lambda i: (i, 0)),
                  pl.BlockSpec(sub_keys.shape, lambda i: (0, 0, 0, 0))],
        out_specs=(bspec, bspec, bspec, bspec, pl.BlockSpec((PEER_HEADS, 8, ts), lambda i: (0, 0, i))),
        compiler_params=_cparams(("arbitrary",)),
        name="peer_select",
    )(q, sub_keys)


PEER_TM = 512
PEER_TE = 512


def _peer_dense_kernel(xn_ref, u_ref, v_ref, s1_ref, s2_ref, e1_ref, e2_ref, thr_ref, o_ref):
    e = pl.program_id(1)
    te = u_ref.shape[0]
    act = jax.nn.gelu(_dot_nt(u_ref[...], xn_ref[...]), approximate=True)
    pieces = []
    for r in range(te // PEER_NKEYS):
        i1 = e * (te // PEER_NKEYS) + r
        g = None
        for h in range(PEER_HEADS):
            s1 = s1_ref[h, pl.ds(i1, 1), :]
            e1 = e1_ref[h, pl.ds(i1, 1), :]
            sel = (s1 + s2_ref[h]) >= thr_ref[h, 0:1, :]
            term = jnp.where(sel, e1 * e2_ref[h], 0.0)
            g = term if g is None else g + term
        pieces.append(g * act[r * PEER_NKEYS:(r + 1) * PEER_NKEYS])
    w = jnp.concatenate(pieces, axis=0)
    contrib = jnp.dot(w.T.astype(BF16), v_ref[...], preferred_element_type=F32)

    @pl.when(e == 0)
    def _():
        o_ref[...] = contrib

    @pl.when(e != 0)
    def _():
        o_ref[...] += contrib


def _peer_dense(xn, u, v, s1, s2, e1, e2, thr):
    S, D = xn.shape
    E = u.shape[0]
    tm, te = min(PEER_TM, S), PEER_TE
    fac = pl.BlockSpec((PEER_HEADS, PEER_NKEYS, tm), lambda i, e: (0, 0, i))
    return pl.pallas_call(
        _peer_dense_kernel,
        out_shape=jax.ShapeDtypeStruct((S, D), F32),
        grid=(S // tm, E // te),
        in_specs=[pl.BlockSpec((tm, D), lambda i, e: (i, 0)),
                  pl.BlockSpec((te, D), lambda i, e: (e, 0)),
                  pl.BlockSpec((te, D), lambda i, e: (e, 0)),
                  fac, fac, fac, fac,
                  pl.BlockSpec((PEER_HEADS, 8, tm), lambda i, e: (0, 0, i))],
        out_specs=pl.BlockSpec((tm, D), lambda i, e: (i, 0)),
        compiler_params=_cparams(("arbitrary", "arbitrary")),
        name="peer_dense",
    )(xn, u, v, s1, s2, e1, e2, thr)


def _final_kernel(h_ref, y_ref, g_ref, o_ref):
    x = h_ref[...] + y_ref[...]
    o_ref[...] = x * lax.rsqrt(jnp.mean(x * x, axis=-1, keepdims=True) + RMS_EPS) * g_ref[...]


def _final(h, y, g, tm):
    S, D = h.shape
    tm = min(tm, S)
    spec = pl.BlockSpec((tm, D), lambda i: (i, 0))
    return pl.pallas_call(
        _final_kernel,
        out_shape=jax.ShapeDtypeStruct((S, D), F32),
        grid=(S // tm,),
        in_specs=[spec, spec, pl.BlockSpec((1, D), lambda i: (0, 0))],
        out_specs=spec,
        compiler_params=_cparams(("arbitrary",)),
        name="final",
    )(h, y, g.reshape(1, D))


def _permute_w_in(w):
    c = _SRC
    sl = lambda a, b: w[:, c[a]:c[b]]
    pad = jnp.zeros((w.shape[0], NP_COLS - COL_TAIL - 96), w.dtype)
    cols = [sl("gv", "gr"), sl("gr", "glow"), sl("dq", "dk"), sl("ga", "gb"), sl("gb", "end"),
            sl("gq", "gk"), sl("gk", "gv"), sl("iq", "ik"), sl("dk", "dv"), sl("dv", "iq"),
            sl("ik", "iw"), sl("glow", "dq"), sl("iw", "ga"), pad]
    return jnp.concatenate(cols, axis=1).astype(BF16)


def _layer(h, mem, positions, norm_mix_g, w_in, gla_gate_w2, gla_gate_b, gla_head_norm_g,
           w_proj_gla, w_proj_dsa, w_out, norm_x_g, norm_mem_g, w_xq, w_xk, w_xv, w_xo,
           norm_ffn_g, peer_wq, peer_sub_keys, peer_u, peer_v):
    proj, _ = _norm_mm(h, norm_mix_g, _permute_w_in(w_in), F32, 1024, 512, "in_proj")
    w2p = jnp.zeros((LANES, GLA_HEADS * GLA_DK), F32).at[TAIL_GLOW:TAIL_GLOW + GLA_GATE_RANK].set(gla_gate_w2)
    o_a = _gla(proj, w2p, gla_gate_b, gla_head_norm_g)
    qb, kb, vb, qi, kia, kib = _rope(proj, positions, 512)
    o_b = _dsa(qi, qb, proj, kia, kib, kb, vb)
    merged = _merge(o_a, o_b, w_proj_gla.astype(BF16), w_proj_dsa.astype(BF16), proj, 1024, 512)
    h = _mm_res(merged, w_out.astype(BF16), h, 1024, 512, "out_proj")

    q_x, _ = _norm_mm(h, norm_x_g, w_xq.astype(BF16), BF16, 1024, 512, "xattn_q")
    kv_x, _ = _norm_mm(mem, norm_mem_g, jnp.concatenate([w_xk, w_xv], axis=1).astype(BF16), BF16, 1024, 512,
                       "xattn_kv")
    o_x = _xattn(q_x, kv_x, 512)
    h = _mm_res(o_x, w_xo.astype(BF16), h, 1024, 512, "xattn_o")

    q_p, xn = _norm_mm(h, norm_ffn_g, peer_wq.astype(BF16), BF16, 1024, 512, "peer_q")
    s1, s2, e1, e2, thr = _peer_select(q_p, peer_sub_keys.astype(BF16))
    y = _peer_dense(xn, peer_u.astype(BF16), peer_v.astype(BF16), s1, s2, e1, e2, thr)
    return h, y


def kernel(x, mem, positions, norm_mix_g, w_in, gla_gate_w2, gla_gate_b, gla_head_norm_g, w_proj_gla, w_proj_dsa, w_out, norm_x_g, norm_mem_g, w_xq, w_xk, w_xv, w_xo, norm_ffn_g, peer_wq, peer_sub_keys, peer_u, peer_v, norm_final_g):
    B, S, D = x.shape
    depth = w_in.shape[0]
    outs = []
    for b in range(B):
        h = x[b]
        y = None
        for l in range(depth):
            if y is not None:
                h = h + y
            h, y = _layer(h, mem[b], positions[b], norm_mix_g[l], w_in[l], gla_gate_w2[l], gla_gate_b[l],
                          gla_head_norm_g[l], w_proj_gla[l], w_proj_dsa[l], w_out[l], norm_x_g[l],
                          norm_mem_g[l], w_xq[l], w_xk[l], w_xv[l], w_xo[l], norm_ffn_g[l], peer_wq[l],
                          peer_sub_keys[l], peer_u[l], peer_v[l])
        outs.append(_final(h, y, norm_final_g, 512))
    return jnp.stack(outs)
```

```python
import functools
import math

import numpy as np
import jax
import jax.numpy as jnp
from jax import lax
from jax.experimental import pallas as pl
from jax.experimental.pallas import tpu as pltpu

F32 = jnp.float32
BF16 = jnp.bfloat16

D_MODEL = 2048
GLA_HEADS = 4
GLA_DK = 256
GLA_DV = 512
GLA_GATE_RANK = 16
GLA_GATE_TEMP = 16.0
DSA_HEADS = 16
DSA_HEAD_DIM = 128
IDX_HEADS = 16
IDX_DIM = 64
DSA_TOPK = 256
XATTN_HEADS = 4
XATTN_HEAD_DIM = D_MODEL // XATTN_HEADS
PEER_HEADS = 8
PEER_NKEYS = 128
PEER_HALF = 128
PEER_TOPK = 16
ROPE_THETA = 10000.0
RMS_EPS = 1e-6

LANES = 128
V7X_VMEM_LIMIT_BYTES = 56 * 1024 * 1024

COL_GV = 0
COL_GR = 2048
COL_DQ = 4096
COL_GA = 6144
COL_GB = 8192
COL_GQ = 10240
COL_GK = 11264
COL_IQ = 12288
COL_DK = 13312
COL_DV = 13440
COL_TAIL = 13568
TAIL_GLOW = 64
TAIL_IW = 80
NP_COLS = 13824

_SRC = dict(gq=0, gk=1024, gv=2048, gr=4096, glow=6144, dq=6160, dk=8208, dv=8336,
            iq=8464, ik=9488, iw=9552, ga=9568, gb=11616, end=13664)


def _cparams(sem, vmem=V7X_VMEM_LIMIT_BYTES):
    return pltpu.CompilerParams(dimension_semantics=sem, vmem_limit_bytes=vmem)


def _dot_nt(a, b):
    return lax.dot_general(a, b, (((1,), (1,)), ((), ())), preferred_element_type=F32)


def _split3(x):
    hi = x.astype(BF16)
    r1 = x - hi.astype(F32)
    mid = r1.astype(BF16)
    lo = (r1 - mid.astype(F32)).astype(BF16)
    return hi, mid, lo


def _norm_mm_kernel(x_ref, g_ref, w_ref, o_ref, xn_ref, a_scr):
    @pl.when(pl.program_id(1) == 0)
    def _():
        x = x_ref[...]
        y = x * lax.rsqrt(jnp.mean(x * x, axis=-1, keepdims=True) + RMS_EPS) * g_ref[...]
        a_scr[...] = y.astype(BF16)
        xn_ref[...] = a_scr[...]

    o_ref[...] = jnp.dot(a_scr[...], w_ref[...], preferred_element_type=F32).astype(o_ref.dtype)


def _norm_mm(x, g, w, out_dtype, tm, tn, name):
    M, K = x.shape
    N = w.shape[1]
    tm, tn = min(tm, M), min(tn, N)
    return pl.pallas_call(
        _norm_mm_kernel,
        out_shape=(jax.ShapeDtypeStruct((M, N), out_dtype), jax.ShapeDtypeStruct((M, K), BF16)),
        grid=(M // tm, N // tn),
        in_specs=[pl.BlockSpec((tm, K), lambda i, j: (i, 0)),
                  pl.BlockSpec((1, K), lambda i, j: (0, 0)),
                  pl.BlockSpec((K, tn), lambda i, j: (0, j))],
        out_specs=(pl.BlockSpec((tm, tn), lambda i, j: (i, j)),
                   pl.BlockSpec((tm, K), lambda i, j: (i, 0))),
        scratch_shapes=[pltpu.VMEM((tm, K), BF16)],
        compiler_params=_cparams(("arbitrary", "arbitrary")),
        name=name,
    )(x, g.reshape(1, K), w)


def _mm_res_kernel(a_ref, w_ref, r_ref, o_ref):
    o_ref[...] = r_ref[...] + jnp.dot(a_ref[...], w_ref[...], preferred_element_type=F32)


def _mm_res(a, w, res, tm, tn, name):
    M, K = a.shape
    N = w.shape[1]
    tm, tn = min(tm, M), min(tn, N)
    return pl.pallas_call(
        _mm_res_kernel,
        out_shape=jax.ShapeDtypeStruct((M, N), F32),
        grid=(M // tm, N // tn),
        in_specs=[pl.BlockSpec((tm, K), lambda i, j: (i, 0)),
                  pl.BlockSpec((K, tn), lambda i, j: (0, j)),
                  pl.BlockSpec((tm, tn), lambda i, j: (i, j))],
        out_specs=pl.BlockSpec((tm, tn), lambda i, j: (i, j)),
        compiler_params=_cparams(("arbitrary", "arbitrary")),
        name=name,
    )(a, w, res)


def _merge_kernel(oa_ref, ob_ref, wa_ref, wb_ref, ga_ref, gb_ref, o_ref):
    ya = jnp.dot(oa_ref[...], wa_ref[...], preferred_element_type=F32)
    yb = jnp.dot(ob_ref[...], wb_ref[...], preferred_element_type=F32)
    o_ref[...] = (jax.nn.sigmoid(ga_ref[...]) * ya + jax.nn.sigmoid(gb_ref[...]) * yb).astype(o_ref.dtype)


def _merge(oa, ob, wa, wb, proj, tm, tn):
    M, K = oa.shape
    N = wa.shape[1]
    tm, tn = min(tm, M), min(tn, N)
    ca, cb = COL_GA // tn, COL_GB // tn
    return pl.pallas_call(
        _merge_kernel,
        out_shape=jax.ShapeDtypeStruct((M, N), BF16),
        grid=(M // tm, N // tn),
        in_specs=[pl.BlockSpec((tm, K), lambda i, j: (i, 0)),
                  pl.BlockSpec((tm, K), lambda i, j: (i, 0)),
                  pl.BlockSpec((K, tn), lambda i, j: (0, j)),
                  pl.BlockSpec((K, tn), lambda i, j: (0, j)),
                  pl.BlockSpec((tm, tn), lambda i, j: (i, ca + j)),
                  pl.BlockSpec((tm, tn), lambda i, j: (i, cb + j))],
        out_specs=pl.BlockSpec((tm, tn), lambda i, j: (i, j)),
        compiler_params=_cparams(("arbitrary", "arbitrary")),
        name="merge",
    )(oa, ob, wa, wb, proj, proj)


GLA_C = 128
GLA_LEVELS = (64, 32, 16)
GLA_DIAG = 16


def _gla_matrices(C):
    t = np.arange(C)[:, None]
    u = np.arange(C)[None, :]
    L = (u <= t).astype(np.float32)
    mats = [L, (u > t).astype(np.float32)]
    for m in GLA_LEVELS:
        anchor = (t // (2 * m)) * (2 * m) + m - 1
        mats.append(L - (u <= anchor).astype(np.float32))
    anchor = (t // GLA_DIAG) * GLA_DIAG
    mats.append(L - (u <= anchor).astype(np.float32))
    return np.concatenate(mats, axis=0)


def _gla_kernel(q_ref, k_ref, v_ref, r_ref, tail_ref, w2_ref, gb_ref, hg_ref, mats_ref, o_ref, st_scr):
    C = GLA_C

    @pl.when(pl.program_id(1) == 0)
    def _():
        st_scr[...] = jnp.zeros_like(st_scr)

    t_hi, t_mid, t_lo = _split3(tail_ref[...])
    w_hi, w_mid, w_lo = _split3(w2_ref[...])
    z = (jnp.dot(t_hi, w_hi, preferred_element_type=F32) + jnp.dot(t_hi, w_mid, preferred_element_type=F32)
         + jnp.dot(t_mid, w_hi, preferred_element_type=F32) + jnp.dot(t_lo, w_hi, preferred_element_type=F32)
         + jnp.dot(t_hi, w_lo, preferred_element_type=F32) + jnp.dot(t_mid, w_mid, preferred_element_type=F32))
    z = z + gb_ref[...]
    log_a = (jnp.minimum(z, 0.0) - jnp.log1p(jnp.exp(-jnp.abs(z)))) * (1.0 / GLA_GATE_TEMP)

    a_hi, a_mid, a_lo = _split3(log_a)
    mats = mats_ref[...]
    dec = (jnp.dot(mats, a_hi, preferred_element_type=F32) + jnp.dot(mats, a_mid, preferred_element_type=F32)
           + jnp.dot(mats, a_lo, preferred_element_type=F32))
    b = dec[0:C]
    b_rest = dec[C:2 * C]

    q = q_ref[...] * (GLA_DK ** -0.5)
    k = k_ref[...]
    v = v_ref[...]
    v_bf = v.astype(BF16)

    row = lax.broadcasted_iota(jnp.int32, (C, C), 0)
    col = lax.broadcasted_iota(jnp.int32, (C, C), 1)
    attn = jnp.zeros((C, C), F32)
    for li, m in enumerate(GLA_LEVELS):
        d = dec[(2 + li) * C:(3 + li) * C]
        qs = (q * jnp.exp(jnp.minimum(d, 0.0))).astype(BF16)
        ks = (k * jnp.exp(jnp.minimum(-d, 0.0))).astype(BF16)
        sh = int(math.log2(2 * m))
        mask = ((row >> sh) == (col >> sh)) & ((row & m) != 0) & ((col & m) == 0)
        attn = attn + jnp.where(mask, _dot_nt(qs, ks), 0.0)
    d = dec[(2 + len(GLA_LEVELS)) * C:(3 + len(GLA_LEVELS)) * C]
    qs = (q * jnp.exp(d)).astype(BF16)
    ks = (k * jnp.exp(-d)).astype(BF16)
    sh = int(math.log2(GLA_DIAG))
    mask = ((row >> sh) == (col >> sh)) & (col <= row)
    attn = attn + jnp.where(mask, _dot_nt(qs, ks), 0.0)

    st = st_scr[...]
    o = _dot_nt((q * jnp.exp(b)).astype(BF16), st.astype(BF16))
    o = o + jnp.dot(attn.astype(BF16), v_bf, preferred_element_type=F32)

    kd = (k * jnp.exp(b_rest)).astype(BF16)
    upd = jnp.dot(v.T.astype(BF16), kd, preferred_element_type=F32)
    st_scr[...] = st * jnp.exp(b[C - 1:C, :]) + upd

    y = o * lax.rsqrt(jnp.mean(o * o, axis=-1, keepdims=True) + RMS_EPS) * hg_ref[...]
    r = r_ref[...]
    o_ref[...] = (y * (r * jax.nn.sigmoid(r))).astype(o_ref.dtype)


def _gla(proj, w2p, gate_b, head_g):
    S = proj.shape[0]
    C = GLA_C
    mats = jnp.asarray(_gla_matrices(C), dtype=BF16)
    nm = mats.shape[0]
    cq, ck = COL_GQ // GLA_DK, COL_GK // GLA_DK
    cv, cr = COL_GV // GLA_DV, COL_GR // GLA_DV
    ct = COL_TAIL // LANES
    return pl.pallas_call(
        _gla_kernel,
        out_shape=jax.ShapeDtypeStruct((S, GLA_HEADS * GLA_DV), BF16),
        grid=(GLA_HEADS, S // C),
        in_specs=[pl.BlockSpec((C, GLA_DK), lambda h, c: (c, cq + h)),
                  pl.BlockSpec((C, GLA_DK), lambda h, c: (c, ck + h)),
                  pl.BlockSpec((C, GLA_DV), lambda h, c: (c, cv + h)),
                  pl.BlockSpec((C, GLA_DV), lambda h, c: (c, cr + h)),
                  pl.BlockSpec((C, LANES), lambda h, c: (c, ct)),
                  pl.BlockSpec((LANES, GLA_DK), lambda h, c: (0, h)),
                  pl.BlockSpec((1, GLA_DK), lambda h, c: (0, h)),
                  pl.BlockSpec((1, GLA_DV), lambda h, c: (0, h)),
                  pl.BlockSpec((nm, C), lambda h, c: (0, 0))],
        out_specs=pl.BlockSpec((C, GLA_DV), lambda h, c: (c, h)),
        scratch_shapes=[pltpu.VMEM((GLA_DV, GLA_DK), F32)],
        compiler_params=_cparams(("arbitrary", "arbitrary")),
        name="gla",
    )(proj, proj, proj, proj, proj, w2p, gate_b.reshape(1, -1), head_g.reshape(1, -1), mats)


def _rope_kernel(pos_ref, dq_ref, dkv_ref, iq_ref, tail_ref, inv_ref, sgn_ref,
                 qb_ref, kb_ref, vb_ref, qi_ref, kia_ref, kib_ref):
    pos = pos_ref[...].astype(F32)
    lane = lax.broadcasted_iota(jnp.int32, (pos.shape[0], LANES), 1)

    ang = pos * inv_ref[0:1, :]
    c128 = jnp.cos(ang)
    s128 = jnp.sin(ang) * sgn_ref[0:1, :]
    ang = pos * inv_ref[1:2, :]
    c64 = jnp.cos(ang)
    s64 = jnp.sin(ang) * sgn_ref[1:2, :]

    def rot128(x):
        return x * c128 + pltpu.roll(x, 64, 1) * s128

    def rot64(x):
        swapped = jnp.where(lane % 64 < 32, pltpu.roll(x, 96, 1), pltpu.roll(x, 32, 1))
        return x * c64 + swapped * s64

    scale = DSA_HEAD_DIM ** -0.5
    for h in range(DSA_HEADS):
        qb_ref[h] = (rot128(dq_ref[:, h * LANES:(h + 1) * LANES]) * scale).astype(BF16)
    kb_ref[...] = rot128(dkv_ref[:, 0:LANES]).astype(BF16)
    vb_ref[:, 0:LANES] = dkv_ref[:, LANES:2 * LANES].astype(BF16)
    vb_ref[:, LANES:2 * LANES] = jnp.ones((pos.shape[0], LANES), BF16)
    for h in range(IDX_HEADS // 2):
        qi_ref[h] = rot64(iq_ref[:, h * LANES:(h + 1) * LANES]).astype(BF16)
    ki = jnp.where(lane < IDX_DIM, rot64(tail_ref[...]), 0.0)
    kia_ref[...] = ki.astype(BF16)
    kib_ref[...] = pltpu.roll(ki, 64, 1).astype(BF16)


def _rope(proj, positions, tm):
    S = proj.shape[0]
    tm = min(tm, S)
    inv128 = ROPE_THETA ** (-jnp.arange(0, DSA_HEAD_DIM, 2, dtype=F32) / DSA_HEAD_DIM)
    inv64 = ROPE_THETA ** (-jnp.arange(0, IDX_DIM, 2, dtype=F32) / IDX_DIM)
    inv = jnp.stack([jnp.tile(inv128, 2), jnp.tile(inv64, 4)])
    sgn = jnp.asarray(np.stack([np.repeat([-1.0, 1.0], 64), np.tile(np.repeat([-1.0, 1.0], 32), 2)]), F32)
    nq = DSA_HEADS * DSA_HEAD_DIM
    ni = IDX_HEADS * IDX_DIM
    outs = (jax.ShapeDtypeStruct((DSA_HEADS, S, LANES), BF16), jax.ShapeDtypeStruct((S, LANES), BF16),
            jax.ShapeDtypeStruct((S, 2 * LANES), BF16), jax.ShapeDtypeStruct((IDX_HEADS // 2, S, LANES), BF16),
            jax.ShapeDtypeStruct((S, LANES), BF16), jax.ShapeDtypeStruct((S, LANES), BF16))
    row = lambda w: pl.BlockSpec((tm, w), lambda i: (i, 0))
    heads = lambda n: pl.BlockSpec((n, tm, LANES), lambda i: (0, i, 0))
    return pl.pallas_call(
        _rope_kernel,
        out_shape=outs,
        grid=(S // tm,),
        in_specs=[pl.BlockSpec((tm, 1), lambda i: (i, 0)),
                  pl.BlockSpec((tm, nq), lambda i: (i, COL_DQ // nq)),
                  pl.BlockSpec((tm, 2 * LANES), lambda i: (i, COL_DK // (2 * LANES))),
                  pl.BlockSpec((tm, ni), lambda i: (i, COL_IQ // ni)),
                  pl.BlockSpec((tm, LANES), lambda i: (i, COL_TAIL // LANES)),
                  pl.BlockSpec((2, LANES), lambda i: (0, 0)),
                  pl.BlockSpec((2, LANES), lambda i: (0, 0))],
        out_specs=(heads(DSA_HEADS), row(LANES), row(2 * LANES), heads(IDX_HEADS // 2), row(LANES), row(LANES)),
        compiler_params=_cparams(("arbitrary",)),
        name="rope",
    )(positions.reshape(S, 1), proj, proj, proj, proj, inv, sgn)


DSA_TQ = 256
DSA_TK = 256
_KEY_ALL = -2139095040


def _sort_key(x):
    bits = lax.bitcast_convert_type(x, jnp.int32)
    return bits ^ ((bits >> 31) & jnp.int32(0x7FFFFFFF))


def _key_to_f32(key):
    return lax.bitcast_convert_type(key ^ ((key >> 31) & jnp.int32(0x7FFFFFFF)), F32)


def _dsa_kernel(topk, qi_ref, qb_ref, tail_ref, kia_ref, kib_ref, kb_ref, va_ref, o_ref,
                key_scr, wbc_scr, smin_scr, smax_scr, m_scr, acc_scr):
    tq, tk = qb_ref.shape[1], DSA_TK
    nhp = IDX_HEADS // 2
    i = pl.program_id(0)
    t0 = i * tq
    nvis = (t0 + tq - 1) // tk + 1

    w_idx = tail_ref[:, TAIL_IW:TAIL_IW + IDX_HEADS] * ((IDX_DIM ** -0.5) * (IDX_HEADS ** -0.5))
    for h in range(IDX_HEADS):
        wbc_scr[h] = jnp.broadcast_to(w_idx[:, h:h + 1], (tq, LANES))
    smin_scr[...] = jnp.full((tq, LANES), jnp.inf, F32)
    smax_scr[...] = jnp.full((tq, LANES), -jnp.inf, F32)
    t_row = t0 + lax.broadcasted_iota(jnp.int32, (tq, LANES), 0)
    lane = lax.broadcasted_iota(jnp.int32, (tq, LANES), 1)

    def score_body(j, carry):
        off = pl.multiple_of(j * tk, tk)
        qi2 = qi_ref[...].reshape(nhp * tq, LANES)
        s_a = _dot_nt(qi2, kia_ref[pl.ds(off, tk), :])
        s_b = _dot_nt(qi2, kib_ref[pl.ds(off, tk), :])
        for c in range(tk // LANES):
            cs = slice(c * LANES, (c + 1) * LANES)
            acc = jnp.zeros((tq, LANES), F32)
            for hp in range(nhp):
                rs = slice(hp * tq, (hp + 1) * tq)
                acc = acc + jnp.maximum(s_a[rs, cs], 0.0) * wbc_scr[2 * hp]
                acc = acc + jnp.maximum(s_b[rs, cs], 0.0) * wbc_scr[2 * hp + 1]
            vis = (off + c * LANES + lane) <= t_row
            key_scr[j, :, cs] = _sort_key(jnp.where(vis, acc, -jnp.inf))
            smin_scr[...] = jnp.minimum(smin_scr[...], jnp.where(vis, acc, jnp.inf))
            smax_scr[...] = jnp.maximum(smax_scr[...], jnp.where(vis, acc, -jnp.inf))
        return carry

    lax.fori_loop(0, nvis, score_body, 0)
    smin = jnp.min(smin_scr[...], axis=1, keepdims=True)
    smax = jnp.max(smax_scr[...], axis=1, keepdims=True)

    n_vis = (t0 + lax.broadcasted_iota(jnp.int32, (tq, 1), 0) + 1).astype(F32)
    select_all = n_vis <= topk
    lo0 = jnp.where(select_all, jnp.int32(_KEY_ALL), _sort_key(smin))
    hi0 = jnp.where(select_all, jnp.int32(_KEY_ALL), _sort_key(smax))

    def count_ge(mid):
        mid_b = jnp.broadcast_to(mid, (tq, LANES))

        def body(j, c):
            for cc in range(tk // LANES):
                c = c + jnp.where(key_scr[j, :, cc * LANES:(cc + 1) * LANES] >= mid_b, 1.0, 0.0)
            return c
        c = lax.fori_loop(0, nvis, body, jnp.zeros((tq, LANES), F32))
        return jnp.sum(c, axis=1, keepdims=True)

    def bis_cond(st):
        lo, hi, c_lo = st
        active = (lo < hi) & (c_lo != topk)
        return jnp.max(jnp.where(active, 1.0, 0.0)) > 0.5

    def bis_body(st):
        lo, hi, c_lo = st
        active = (lo < hi) & (c_lo != topk)
        mid_f = 0.5 * _key_to_f32(lo) + 0.5 * _key_to_f32(hi)
        mid = jnp.clip(_sort_key(mid_f), lo + 1, hi)
        mid = jnp.where(active, mid, lo)
        c = count_ge(mid)
        ge = c >= topk
        lo_n = jnp.where(active & ge, mid, lo)
        c_n = jnp.where(active & ge, c, c_lo)
        hi_n = jnp.where(active & jnp.logical_not(ge), mid - 1, hi)
        return lo_n, hi_n, c_n

    thr, _, _ = lax.while_loop(bis_cond, bis_body, (lo0, hi0, n_vis))

    m_scr[...] = jnp.full(m_scr.shape, -1e30, F32)
    acc_scr[...] = jnp.zeros(acc_scr.shape, F32)
    thr_b = jnp.broadcast_to(thr, (tq, tk))

    def attn_body(j, carry):
        off = pl.multiple_of(j * tk, tk)
        bias = jnp.where(key_scr[j] >= thr_b, 0.0, -jnp.inf)
        s = _dot_nt(qb_ref[...].reshape(DSA_HEADS * tq, LANES), kb_ref[pl.ds(off, tk), :])
        s = (s.reshape(DSA_HEADS, tq, tk) + bias[None]).reshape(DSA_HEADS * tq, tk)
        m_old = m_scr[...]
        m_new = jnp.maximum(m_old, jnp.max(s, axis=1, keepdims=True))
        p = jnp.exp(s - jnp.concatenate([m_new] * (tk // LANES), axis=1))
        pv = jnp.dot(p.astype(BF16), va_ref[pl.ds(off, tk), :], preferred_element_type=F32)
        alpha = jnp.exp(m_old - m_new)
        acc_scr[...] = jnp.concatenate([alpha, alpha], axis=1) * acc_scr[...] + pv
        m_scr[...] = m_new
        return carry

    lax.fori_loop(0, nvis, attn_body, 0)
    for h in range(DSA_HEADS):
        rs = slice(h * tq, (h + 1) * tq)
        o_ref[:, h * LANES:(h + 1) * LANES] = (acc_scr[rs, 0:LANES] / acc_scr[rs, LANES:2 * LANES]).astype(o_ref.dtype)


def _dsa(qi, qb, proj, kia, kib, kb, va):
    S = kb.shape[0]
    tq, tk = min(DSA_TQ, S), DSA_TK
    topk = min(DSA_TOPK, S // 4)
    nq = DSA_HEADS * DSA_HEAD_DIM
    full = lambda w: pl.BlockSpec((S, w), lambda i: (0, 0), pipeline_mode=pl.Buffered(1))
    return pl.pallas_call(
        functools.partial(_dsa_kernel, topk),
        out_shape=jax.ShapeDtypeStruct((S, nq), BF16),
        grid=(S // tq,),
        in_specs=[pl.BlockSpec((IDX_HEADS // 2, tq, LANES), lambda i: (0, i, 0)),
                  pl.BlockSpec((DSA_HEADS, tq, LANES), lambda i: (0, i, 0)),
                  pl.BlockSpec((tq, LANES), lambda i: (i, COL_TAIL // LANES)),
                  full(LANES), full(LANES), full(LANES), full(2 * LANES)],
        out_specs=pl.BlockSpec((tq, nq), lambda i: (i, 0)),
        scratch_shapes=[pltpu.VMEM((S // tk, tq, tk), jnp.int32),
                        pltpu.VMEM((IDX_HEADS, tq, LANES), F32),
                        pltpu.VMEM((tq, LANES), F32),
                        pltpu.VMEM((tq, LANES), F32),
                        pltpu.VMEM((DSA_HEADS * tq, LANES), F32),
                        pltpu.VMEM((DSA_HEADS * tq, 2 * LANES), F32)],
        compiler_params=_cparams(("arbitrary",)),
        name="dsa",
    )(qi, qb, proj, kia, kib, kb, va)


def _xattn_kernel(q_ref, kv_ref, o_ref):
    d = XATTN_HEAD_DIM
    nkv = XATTN_HEADS * d
    for h in range(XATTN_HEADS):
        q = q_ref[:, h * d:(h + 1) * d]
        k = kv_ref[:, h * d:(h + 1) * d]
        v = kv_ref[:, nkv + h * d:nkv + (h + 1) * d]
        s = _dot_nt(q, k) * (d ** -0.5)
        s = s - jnp.max(s, axis=1, keepdims=True)
        p = jnp.exp(s)
        p = p / jnp.sum(p, axis=1, keepdims=True)
        o_ref[:, h * d:(h + 1) * d] = jnp.dot(p.astype(BF16), v, preferred_element_type=F32).astype(o_ref.dtype)


def _xattn(q, kv, tm):
    S, N = q.shape
    M = kv.shape[0]
    tm = min(tm, S)
    return pl.pallas_call(
        _xattn_kernel,
        out_shape=jax.ShapeDtypeStruct((S, N), BF16),
        grid=(S // tm,),
        in_specs=[pl.BlockSpec((tm, N), lambda i: (i, 0)),
                  pl.BlockSpec((M, 2 * N), lambda i: (0, 0))],
        out_specs=pl.BlockSpec((tm, N), lambda i: (i, 0)),
        compiler_params=_cparams(("arbitrary",)),
        name="xattn",
    )(q, kv)


PEER_TS = 256


def _top16_distinct(s):
    vals, cnts = [], []
    rem = s
    for _ in range(PEER_TOPK):
        m = jnp.max(rem, axis=0, keepdims=True)
        eq = rem == m
        cnts.append(jnp.sum(jnp.where(eq, 1.0, 0.0), axis=0, keepdims=True))
        vals.append(m)
        rem = jnp.where(eq, -jnp.inf, rem)
    return vals, cnts


def _stack_rows(rows):
    n, T = len(rows), rows[0].shape[1]
    idx = lax.broadcasted_iota(jnp.int32, (n, T), 0)
    out = jnp.broadcast_to(rows[0], (n, T))
    for r in range(1, n):
        out = jnp.where(idx == r, rows[r], out)
    return out


def _peer_select_kernel(q_ref, sk_ref, s1_ref, s2_ref, e1_ref, e2_ref, thr_ref):
    T = q_ref.shape[0]
    for h in range(PEER_HEADS):
        s1 = _dot_nt(sk_ref[h, 0], q_ref[:, (2 * h) * PEER_HALF:(2 * h + 1) * PEER_HALF])
        s2 = _dot_nt(sk_ref[h, 1], q_ref[:, (2 * h + 1) * PEER_HALF:(2 * h + 2) * PEER_HALF])
        a_vals, a_cnts = _top16_distinct(s1)
        b_vals, b_cnts = _top16_distinct(s2)
        b_mat = _stack_rows(b_vals)
        bc_mat = _stack_rows(b_cnts)
        cand = jnp.concatenate([a + b_mat for a in a_vals], axis=0)
        wgt = jnp.concatenate([c * bc_mat for c in a_cnts], axis=0)
        rem = cand
        n = jnp.zeros((1, T), F32)
        thr = jnp.full((1, T), -jnp.inf, F32)
        for _ in range(PEER_TOPK):
            m = jnp.max(rem, axis=0, keepdims=True)
            eq = rem == m
            thr = jnp.where(n < PEER_TOPK, m, thr)
            n = n + jnp.sum(jnp.where(eq, wgt, 0.0), axis=0, keepdims=True)
            rem = jnp.where(eq, -jnp.inf, rem)
        vmax = a_vals[0] + b_vals[0]
        z = jnp.sum(jnp.where(cand >= thr, wgt * jnp.exp(cand - vmax), 0.0), axis=0, keepdims=True)
        s1_ref[h] = s1
        s2_ref[h] = s2
        e1_ref[h] = jnp.exp(s1 - a_vals[0])
        e2_ref[h] = jnp.exp(s2 - b_vals[0]) / z
        thr_ref[h] = jnp.broadcast_to(thr, (8, T))


def _peer_select(q, sub_keys):
    S = q.shape[0]
    ts = min(PEER_TS, S)
    big = jax.ShapeDtypeStruct((PEER_HEADS, PEER_NKEYS, S), F32)
    bspec = pl.BlockSpec((PEER_HEADS, PEER_NKEYS, ts), lambda i: (0, 0, i))
    return pl.pallas_call(
        _peer_select_kernel,
        out_shape=(big, big, big, big, jax.ShapeDtypeStruct((PEER_HEADS, 8, S), F32)),
        grid=(S // ts,),
        in_specs=[pl.BlockSpec((ts, q.shape[1]), lambda i: (i, 0)),
                  pl.BlockSpec(sub_keys.shape, lambda i: (0, 0, 0, 0))],
        out_specs=(bspec, bspec, bspec, bspec, pl.BlockSpec((PEER_HEADS, 8, ts), lambda i: (0, 0, i))),
        compiler_params=_cparams(("arbitrary",)),
        name="peer_select",
    )(q, sub_keys)


PEER_TM = 512
PEER_TE = 512


def _peer_dense_kernel(xn_ref, u_ref, v_ref, s1_ref, s2_ref, e1_ref, e2_ref, thr_ref, o_ref):
    e = pl.program_id(1)
    te = u_ref.shape[0]
    act = jax.nn.gelu(_dot_nt(u_ref[...], xn_ref[...]), approximate=True)
    pieces = []
    for r in range(te // PEER_NKEYS):
        i1 = e * (te // PEER_NKEYS) + r
        g = None
        for h in range(PEER_HEADS):
            s1 = s1_ref[h, pl.ds(i1, 1), :]
            e1 = e1_ref[h, pl.ds(i1, 1), :]
            sel = (s1 + s2_ref[h]) >= thr_ref[h, 0:1, :]
            term = jnp.where(sel, e1 * e2_ref[h], 0.0)
            g = term if g is None else g + term
        pieces.append(g * act[r * PEER_NKEYS:(r + 1) * PEER_NKEYS])
    w = jnp.concatenate(pieces, axis=0)
    contrib = jnp.dot(w.T.astype(BF16), v_ref[...], preferred_element_type=F32)

    @pl.when(e == 0)
    def _():
        o_ref[...] = contrib

    @pl.when(e != 0)
    def _():
        o_ref[...] += contrib


def _peer_dense(xn, u, v, s1, s2, e1, e2, thr):
    S, D = xn.shape
    E = u.shape[0]
    tm, te = min(PEER_TM, S), PEER_TE
    fac = pl.BlockSpec((PEER_HEADS, PEER_NKEYS, tm), lambda i, e: (0, 0, i))
    return pl.pallas_call(
        _peer_dense_kernel,
        out_shape=jax.ShapeDtypeStruct((S, D), F32),
        grid=(S // tm, E // te),
        in_specs=[pl.BlockSpec((tm, D), lambda i, e: (i, 0)),
                  pl.BlockSpec((te, D), lambda i, e: (e, 0)),
                  pl.BlockSpec((te, D), lambda i, e: (e, 0)),
                  fac, fac, fac, fac,
                  pl.BlockSpec((PEER_HEADS, 8, tm), lambda i, e: (0, 0, i))],
        out_specs=pl.BlockSpec((tm, D), lambda i, e: (i, 0)),
        compiler_params=_cparams(("arbitrary", "arbitrary")),
        name="peer_dense",
    )(xn, u, v, s1, s2, e1, e2, thr)


def _final_kernel(h_ref, y_ref, g_ref, o_ref):
    x = h_ref[...] + y_ref[...]
    o_ref[...] = x * lax.rsqrt(jnp.mean(x * x, axis=-1, keepdims=True) + RMS_EPS) * g_ref[...]


def _final(h, y, g, tm):
    S, D = h.shape
    tm = min(tm, S)
    spec = pl.BlockSpec((tm, D), lambda i: (i, 0))
    return pl.pallas_call(
        _final_kernel,
        out_shape=jax.ShapeDtypeStruct((S, D), F32),
        grid=(S // tm,),
        in_specs=[spec, spec, pl.BlockSpec((1, D), lambda i: (0, 0))],
        out_specs=spec,
        compiler_params=_cparams(("arbitrary",)),
        name="final",
    )(h, y, g.reshape(1, D))


def _permute_w_in(w):
    c = _SRC
    sl = lambda a, b: w[:, c[a]:c[b]]
    pad = jnp.zeros((w.shape[0], NP_COLS - COL_TAIL - 96), w.dtype)
    cols = [sl("gv", "gr"), sl("gr", "glow"), sl("dq", "dk"), sl("ga", "gb"), sl("gb", "end"),
            sl("gq", "gk"), sl("gk", "gv"), sl("iq", "ik"), sl("dk", "dv"), sl("dv", "iq"),
            sl("ik", "iw"), sl("glow", "dq"), sl("iw", "ga"), pad]
    return jnp.concatenate(cols, axis=1).astype(BF16)


def _layer(h, mem, positions, norm_mix_g, w_in, gla_gate_w2, gla_gate_b, gla_head_norm_g,
           w_proj_gla, w_proj_dsa, w_out, norm_x_g, norm_mem_g, w_xq, w_xk, w_xv, w_xo,
           norm_ffn_g, peer_wq, peer_sub_keys, peer_u, peer_v):
    proj, _ = _norm_mm(h, norm_mix_g, _permute_w_in(w_in), F32, 1024, 512, "in_proj")
    w2p = jnp.zeros((LANES, GLA_HEADS * GLA_DK), F32).at[TAIL_GLOW:TAIL_GLOW + GLA_GATE_RANK].set(gla_gate_w2)
    o_a = _gla(proj, w2p, gla_gate_b, gla_head_norm_g)
    qb, kb, vb, qi, kia, kib = _rope(proj, positions, 512)
    o_b = _dsa(qi, qb, proj, kia, kib, kb, vb)
    merged = _merge(o_a, o_b, w_proj_gla.astype(BF16), w_proj_dsa.astype(BF16), proj, 1024, 512)
    h = _mm_res(merged, w_out.astype(BF16), h, 1024, 512, "out_proj")

    q_x, _ = _norm_mm(h, norm_x_g, w_xq.astype(BF16), BF16, 1024, 512, "xattn_q")
    kv_x, _ = _norm_mm(mem, norm_mem_g, jnp.concatenate([w_xk, w_xv], axis=1).astype(BF16), BF16, 1024, 512,
                       "xattn_kv")
    o_x = _xattn(q_x, kv_x, 512)
    h = _mm_res(o_x, w_xo.astype(BF16), h, 1024, 512, "xattn_o")

    q_p, xn = _norm_mm(h, norm_ffn_g, peer_wq.astype(BF16), BF16, 1024, 512, "peer_q")
    s1, s2, e1, e2, thr = _peer_select(q_p, peer_sub_keys.astype(BF16))
    y = _peer_dense(xn, peer_u.astype(BF16), peer_v.astype(BF16), s1, s2, e1, e2, thr)
    return h, y


def kernel(x, mem, positions, norm_mix_g, w_in, gla_gate_w2, gla_gate_b, gla_head_norm_g, w_proj_gla, w_proj_dsa, w_out, norm_x_g, norm_mem_g, w_xq, w_xk, w_xv, w_xo, norm_ffn_g, peer_wq, peer_sub_keys, peer_u, peer_v, norm_final_g):
    B, S, D = x.shape
    depth = w_in.shape[0]
    outs = []
    for b in range(B):
        h = x[b]
        y = None
        for l in range(depth):
            if y is not None:
                h = h + y
            h, y = _layer(h, mem[b], positions[b], norm_mix_g[l], w_in[l], gla_gate_w2[l], gla_gate_b[l],
                          gla_head_norm_g[l], w_proj_gla[l], w_proj_dsa[l], w_out[l], norm_x_g[l],
                          norm_mem_g[l], w_xq[l], w_xk[l], w_xv[l], w_xo[l], norm_ffn_g[l], peer_wq[l],
                          peer_sub_keys[l], peer_u[l], peer_v[l])
        outs.append(_final(h, y, norm_final_g, 512))
    return jnp.stack(outs)
```

```python
import functools
import math

import numpy as np
import jax
import jax.numpy as jnp
from jax import lax
from jax.experimental import pallas as pl
from jax.experimental.pallas import tpu as pltpu

F32 = jnp.float32
BF16 = jnp.bfloat16

D_MODEL = 2048
GLA_HEADS = 4
GLA_DK = 256
GLA_DV = 512
GLA_GATE_RANK = 16
GLA_GATE_TEMP = 16.0
DSA_HEADS = 16
DSA_HEAD_DIM = 128
IDX_HEADS = 16
IDX_DIM = 64
DSA_TOPK = 256
XATTN_HEADS = 4
XATTN_HEAD_DIM = D_MODEL // XATTN_HEADS
PEER_HEADS = 8
PEER_NKEYS = 128
PEER_HALF = 128
PEER_TOPK = 16
ROPE_THETA = 10000.0
RMS_EPS = 1e-6

LANES = 128
V7X_VMEM_LIMIT_BYTES = 56 * 1024 * 1024

COL_GV = 0
COL_GR = 2048
COL_DQ = 4096
COL_GA = 6144
COL_GB = 8192
COL_GQ = 10240
COL_GK = 11264
COL_IQ = 12288
COL_DK = 13312
COL_DV = 13440
COL_TAIL = 13568
TAIL_GLOW = 64
TAIL_IW = 80
NP_COLS = 13824

_SRC = dict(gq=0, gk=1024, gv=2048, gr=4096, glow=6144, dq=6160, dk=8208, dv=8336,
            iq=8464, ik=9488, iw=9552, ga=9568, gb=11616, end=13664)


def _cparams(sem, vmem=V7X_VMEM_LIMIT_BYTES):
    return pltpu.CompilerParams(dimension_semantics=sem, vmem_limit_bytes=vmem)


def _dot_nt(a, b):
    return lax.dot_general(a, b, (((1,), (1,)), ((), ())), preferred_element_type=F32)


def _split3(x):
    hi = x.astype(BF16)
    r1 = x - hi.astype(F32)
    mid = r1.astype(BF16)
    lo = (r1 - mid.astype(F32)).astype(BF16)
    return hi, mid, lo


def _norm_mm_kernel(keep_xn, x_ref, g_ref, w_ref, o_ref, *rest):
    a_scr = rest[-1]

    @pl.when(pl.program_id(1) == 0)
    def _():
        x = x_ref[...]
        y = x * lax.rsqrt(jnp.mean(x * x, axis=-1, keepdims=True) + RMS_EPS) * g_ref[...]
        a_scr[...] = y.astype(BF16)
        if keep_xn:
            rest[0][...] = a_scr[...]

    o_ref[...] = jnp.dot(a_scr[...], w_ref[...], preferred_element_type=F32).astype(o_ref.dtype)


def _norm_mm(x, g, w, out_dtype, tm, tn, name, keep_xn=False):
    M, K = x.shape
    N = w.shape[1]
    tm, tn = min(tm, M), min(tn, N)
    out_shape = [jax.ShapeDtypeStruct((M, N), out_dtype)]
    out_specs = [pl.BlockSpec((tm, tn), lambda i, j: (i, j))]
    if keep_xn:
        out_shape.append(jax.ShapeDtypeStruct((M, K), BF16))
        out_specs.append(pl.BlockSpec((tm, K), lambda i, j: (i, 0)))
    return pl.pallas_call(
        functools.partial(_norm_mm_kernel, keep_xn),
        out_shape=tuple(out_shape),
        grid=(M // tm, N // tn),
        in_specs=[pl.BlockSpec((tm, K), lambda i, j: (i, 0)),
                  pl.BlockSpec((1, K), lambda i, j: (0, 0)),
                  pl.BlockSpec((K, tn), lambda i, j: (0, j))],
        out_specs=tuple(out_specs),
        scratch_shapes=[pltpu.VMEM((tm, K), BF16)],
        compiler_params=_cparams(("arbitrary", "arbitrary")),
        name=name,
    )(x, g.reshape(1, K), w)


def _mm_res_kernel(a_ref, w_ref, r_ref, o_ref):
    o_ref[...] = r_ref[...] + jnp.dot(a_ref[...], w_ref[...], preferred_element_type=F32)


def _mm_res(a, w, res, tm, tn, name):
    M, K = a.shape
    N = w.shape[1]
    tm, tn = min(tm, M), min(tn, N)
    return pl.pallas_call(
        _mm_res_kernel,
        out_shape=jax.ShapeDtypeStruct((M, N), F32),
        grid=(M // tm, N // tn),
        in_specs=[pl.BlockSpec((tm, K), lambda i, j: (i, 0)),
                  pl.BlockSpec((K, tn), lambda i, j: (0, j)),
                  pl.BlockSpec((tm, tn), lambda i, j: (i, j))],
        out_specs=pl.BlockSpec((tm, tn), lambda i, j: (i, j)),
        compiler_params=_cparams(("arbitrary", "arbitrary")),
        name=name,
    )(a, w, res)


def _merge_kernel(oa_ref, ob_ref, wa_ref, wb_ref, ga_ref, gb_ref, o_ref):
    ya = jnp.dot(oa_ref[...], wa_ref[...], preferred_element_type=F32)
    yb = jnp.dot(ob_ref[...], wb_ref[...], preferred_element_type=F32)
    o_ref[...] = (jax.nn.sigmoid(ga_ref[...]) * ya + jax.nn.sigmoid(gb_ref[...]) * yb).astype(o_ref.dtype)


def _merge(oa, ob, wa, wb, proj, tm, tn):
    M, K = oa.shape
    N = wa.shape[1]
    tm, tn = min(tm, M), min(tn, N)
    ca, cb = COL_GA // tn, COL_GB // tn
    return pl.pallas_call(
        _merge_kernel,
        out_shape=jax.ShapeDtypeStruct((M, N), BF16),
        grid=(M // tm, N // tn),
        in_specs=[pl.BlockSpec((tm, K), lambda i, j: (i, 0)),
                  pl.BlockSpec((tm, K), lambda i, j: (i, 0)),
                  pl.BlockSpec((K, tn), lambda i, j: (0, j)),
                  pl.BlockSpec((K, tn), lambda i, j: (0, j)),
                  pl.BlockSpec((tm, tn), lambda i, j: (i, ca + j)),
                  pl.BlockSpec((tm, tn), lambda i, j: (i, cb + j))],
        out_specs=pl.BlockSpec((tm, tn), lambda i, j: (i, j)),
        compiler_params=_cparams(("arbitrary", "arbitrary")),
        name="merge",
    )(oa, ob, wa, wb, proj, proj)


GLA_C = 128
GLA_LEVELS = (64, 32, 16)
GLA_DIAG = 16


def _gla_matrices(C):
    t = np.arange(C)[:, None]
    u = np.arange(C)[None, :]
    L = (u <= t).astype(np.float32)
    mats = [L, (u > t).astype(np.float32)]
    for m in GLA_LEVELS:
        anchor = (t // (2 * m)) * (2 * m) + m - 1
        mats.append(L - (u <= anchor).astype(np.float32))
    anchor = (t // GLA_DIAG) * GLA_DIAG
    mats.append(L - (u <= anchor).astype(np.float32))
    return np.concatenate(mats, axis=0)


def _gla_kernel(q_ref, k_ref, v_ref, r_ref, tail_ref, w2_ref, gb_ref, hg_ref, mats_ref, o_ref, st_scr):
    C = GLA_C

    @pl.when(pl.program_id(1) == 0)
    def _():
        st_scr[...] = jnp.zeros_like(st_scr)

    t_hi, t_mid, t_lo = _split3(tail_ref[...])
    w_hi, w_mid, w_lo = _split3(w2_ref[...])
    z = (jnp.dot(t_hi, w_hi, preferred_element_type=F32) + jnp.dot(t_hi, w_mid, preferred_element_type=F32)
         + jnp.dot(t_mid, w_hi, preferred_element_type=F32) + jnp.dot(t_lo, w_hi, preferred_element_type=F32)
         + jnp.dot(t_hi, w_lo, preferred_element_type=F32) + jnp.dot(t_mid, w_mid, preferred_element_type=F32))
    z = z + gb_ref[...]
    log_a = (jnp.minimum(z, 0.0) - jnp.log1p(jnp.exp(-jnp.abs(z)))) * (1.0 / GLA_GATE_TEMP)

    a_hi, a_mid, a_lo = _split3(log_a)
    mats = mats_ref[...]
    dec = (jnp.dot(mats, a_hi, preferred_element_type=F32) + jnp.dot(mats, a_mid, preferred_element_type=F32)
           + jnp.dot(mats, a_lo, preferred_element_type=F32))
    b = dec[0:C]
    b_rest = dec[C:2 * C]

    q = q_ref[...] * (GLA_DK ** -0.5)
    k = k_ref[...]
    v = v_ref[...]
    v_bf = v.astype(BF16)

    row = lax.broadcasted_iota(jnp.int32, (C, C), 0)
    col = lax.broadcasted_iota(jnp.int32, (C, C), 1)
    attn = jnp.zeros((C, C), F32)
    for li, m in enumerate(GLA_LEVELS):
        d = dec[(2 + li) * C:(3 + li) * C]
        qs = (q * jnp.exp(jnp.minimum(d, 0.0))).astype(BF16)
        ks = (k * jnp.exp(jnp.minimum(-d, 0.0))).astype(BF16)
        sh = int(math.log2(2 * m))
        mask = ((row >> sh) == (col >> sh)) & ((row & m) != 0) & ((col & m) == 0)
        attn = attn + jnp.where(mask, _dot_nt(qs, ks), 0.0)
    d = dec[(2 + len(GLA_LEVELS)) * C:(3 + len(GLA_LEVELS)) * C]
    qs = (q * jnp.exp(d)).astype(BF16)
    ks = (k * jnp.exp(-d)).astype(BF16)
    sh = int(math.log2(GLA_DIAG))
    mask = ((row >> sh) == (col >> sh)) & (col <= row)
    attn = attn + jnp.where(mask, _dot_nt(qs, ks), 0.0)

    st = st_scr[...]
    o = _dot_nt((q * jnp.exp(b)).astype(BF16), st.astype(BF16))
    o = o + jnp.dot(attn.astype(BF16), v_bf, preferred_element_type=F32)

    kd = (k * jnp.exp(b_rest)).astype(BF16)
    upd = jnp.dot(v.T.astype(BF16), kd, preferred_element_type=F32)
    st_scr[...] = st * jnp.exp(b[C - 1:C, :]) + upd

    y = o * lax.rsqrt(jnp.mean(o * o, axis=-1, keepdims=True) + RMS_EPS) * hg_ref[...]
    r = r_ref[...]
    o_ref[...] = (y * (r * jax.nn.sigmoid(r))).astype(o_ref.dtype)


def _gla(proj, w2p, gate_b, head_g):
    S = proj.shape[0]
    C = GLA_C
    mats = jnp.asarray(_gla_matrices(C), dtype=BF16)
    nm = mats.shape[0]
    cq, ck = COL_GQ // GLA_DK, COL_GK // GLA_DK
    cv, cr = COL_GV // GLA_DV, COL_GR // GLA_DV
    ct = COL_TAIL // LANES
    return pl.pallas_call(
        _gla_kernel,
        out_shape=jax.ShapeDtypeStruct((S, GLA_HEADS * GLA_DV), BF16),
        grid=(GLA_HEADS, S // C),
        in_specs=[pl.BlockSpec((C, GLA_DK), lambda h, c: (c, cq + h)),
                  pl.BlockSpec((C, GLA_DK), lambda h, c: (c, ck + h)),
                  pl.BlockSpec((C, GLA_DV), lambda h, c: (c, cv + h)),
                  pl.BlockSpec((C, GLA_DV), lambda h, c: (c, cr + h)),
                  pl.BlockSpec((C, LANES), lambda h, c: (c, ct)),
                  pl.BlockSpec((LANES, GLA_DK), lambda h, c: (0, h)),
                  pl.BlockSpec((1, GLA_DK), lambda h, c: (0, h)),
                  pl.BlockSpec((1, GLA_DV), lambda h, c: (0, h)),
                  pl.BlockSpec((nm, C), lambda h, c: (0, 0))],
        out_specs=pl.BlockSpec((C, GLA_DV), lambda h, c: (c, h)),
        scratch_shapes=[pltpu.VMEM((GLA_DV, GLA_DK), F32)],
        compiler_params=_cparams(("arbitrary", "arbitrary")),
        name="gla",
    )(proj, proj, proj, proj, proj, w2p, gate_b.reshape(1, -1), head_g.reshape(1, -1), mats)


def _rope_kernel(pos_ref, dq_ref, dkv_ref, iq_ref, tail_ref, inv_ref, sgn_ref,
                 qb_ref, kb_ref, vb_ref, qi_ref, kia_ref, kib_ref):
    pos = pos_ref[...].astype(F32)
    lane = lax.broadcasted_iota(jnp.int32, (pos.shape[0], LANES), 1)

    ang = pos * inv_ref[0:1, :]
    c128 = jnp.cos(ang)
    s128 = jnp.sin(ang) * sgn_ref[0:1, :]
    ang = pos * inv_ref[1:2, :]
    c64 = jnp.cos(ang)
    s64 = jnp.sin(ang) * sgn_ref[1:2, :]

    def rot128(x):
        return x * c128 + pltpu.roll(x, 64, 1) * s128

    def rot64(x):
        swapped = jnp.where(lane % 64 < 32, pltpu.roll(x, 96, 1), pltpu.roll(x, 32, 1))
        return x * c64 + swapped * s64

    scale = DSA_HEAD_DIM ** -0.5
    for h in range(DSA_HEADS):
        qb_ref[h] = (rot128(dq_ref[:, h * LANES:(h + 1) * LANES]) * scale).astype(BF16)
    kb_ref[...] = rot128(dkv_ref[:, 0:LANES]).astype(BF16)
    vb_ref[:, 0:LANES] = dkv_ref[:, LANES:2 * LANES].astype(BF16)
    vb_ref[:, LANES:2 * LANES] = jnp.ones((pos.shape[0], LANES), BF16)
    for h in range(IDX_HEADS // 2):
        qi_ref[h] = rot64(iq_ref[:, h * LANES:(h + 1) * LANES]).astype(BF16)
    ki = jnp.where(lane < IDX_DIM, rot64(tail_ref[...]), 0.0)
    kia_ref[...] = ki.astype(BF16)
    kib_ref[...] = pltpu.roll(ki, 64, 1).astype(BF16)


def _rope(proj, positions, tm):
    S = proj.shape[0]
    tm = min(tm, S)
    inv128 = ROPE_THETA ** (-jnp.arange(0, DSA_HEAD_DIM, 2, dtype=F32) / DSA_HEAD_DIM)
    inv64 = ROPE_THETA ** (-jnp.arange(0, IDX_DIM, 2, dtype=F32) / IDX_DIM)
    inv = jnp.stack([jnp.tile(inv128, 2), jnp.tile(inv64, 4)])
    sgn = jnp.asarray(np.stack([np.repeat([-1.0, 1.0], 64), np.tile(np.repeat([-1.0, 1.0], 32), 2)]), F32)
    nq = DSA_HEADS * DSA_HEAD_DIM
    ni = IDX_HEADS * IDX_DIM
    outs = (jax.ShapeDtypeStruct((DSA_HEADS, S, LANES), BF16), jax.ShapeDtypeStruct((S, LANES), BF16),
            jax.ShapeDtypeStruct((S, 2 * LANES), BF16), jax.ShapeDtypeStruct((IDX_HEADS // 2, S, LANES), BF16),
            jax.ShapeDtypeStruct((S, LANES), BF16), jax.ShapeDtypeStruct((S, LANES), BF16))
    row = lambda w: pl.BlockSpec((tm, w), lambda i: (i, 0))
    heads = lambda n: pl.BlockSpec((n, tm, LANES), lambda i: (0, i, 0))
    return pl.pallas_call(
        _rope_kernel,
        out_shape=outs,
        grid=(S // tm,),
        in_specs=[pl.BlockSpec((tm, 1), lambda i: (i, 0)),
                  pl.BlockSpec((tm, nq), lambda i: (i, COL_DQ // nq)),
                  pl.BlockSpec((tm, 2 * LANES), lambda i: (i, COL_DK // (2 * LANES))),
                  pl.BlockSpec((tm, ni), lambda i: (i, COL_IQ // ni)),
                  pl.BlockSpec((tm, LANES), lambda i: (i, COL_TAIL // LANES)),
                  pl.BlockSpec((2, LANES), lambda i: (0, 0)),
                  pl.BlockSpec((2, LANES), lambda i: (0, 0))],
        out_specs=(heads(DSA_HEADS), row(LANES), row(2 * LANES), heads(IDX_HEADS // 2), row(LANES), row(LANES)),
        compiler_params=_cparams(("arbitrary",)),
        name="rope",
    )(positions.reshape(S, 1), proj, proj, proj, proj, inv, sgn)


DSA_TQ = 256
DSA_TK = 256
_KEY_ALL = -2139095040


def _sort_key(x):
    bits = lax.bitcast_convert_type(x, jnp.int32)
    return bits ^ ((bits >> 31) & jnp.int32(0x7FFFFFFF))


def _key_to_f32(key):
    return lax.bitcast_convert_type(key ^ ((key >> 31) & jnp.int32(0x7FFFFFFF)), F32)


def _dsa_kernel(topk, qi_ref, qb_ref, tail_ref, kia_ref, kib_ref, kb_ref, va_ref, o_ref,
                key_scr, wbc_scr, smin_scr, smax_scr, m_scr, acc_scr):
    tq, tk = qb_ref.shape[1], DSA_TK
    nhp = IDX_HEADS // 2
    i = pl.program_id(0)
    t0 = i * tq
    nvis = (t0 + tq - 1) // tk + 1

    w_idx = tail_ref[:, TAIL_IW:TAIL_IW + IDX_HEADS] * ((IDX_DIM ** -0.5) * (IDX_HEADS ** -0.5))
    for h in range(IDX_HEADS):
        wbc_scr[h] = jnp.broadcast_to(w_idx[:, h:h + 1], (tq, LANES))
    smin_scr[...] = jnp.full((tq, LANES), jnp.inf, F32)
    smax_scr[...] = jnp.full((tq, LANES), -jnp.inf, F32)
    t_row = t0 + lax.broadcasted_iota(jnp.int32, (tq, LANES), 0)
    lane = lax.broadcasted_iota(jnp.int32, (tq, LANES), 1)

    def score_body(j, carry):
        off = pl.multiple_of(j * tk, tk)
        qi2 = qi_ref[...].reshape(nhp * tq, LANES)
        s_a = _dot_nt(qi2, kia_ref[pl.ds(off, tk), :])
        s_b = _dot_nt(qi2, kib_ref[pl.ds(off, tk), :])
        for c in range(tk // LANES):
            cs = slice(c * LANES, (c + 1) * LANES)
            acc = jnp.zeros((tq, LANES), F32)
            for hp in range(nhp):
                rs = slice(hp * tq, (hp + 1) * tq)
                acc = acc + jnp.maximum(s_a[rs, cs], 0.0) * wbc_scr[2 * hp]
                acc = acc + jnp.maximum(s_b[rs, cs], 0.0) * wbc_scr[2 * hp + 1]
            vis = (off + c * LANES + lane) <= t_row
            key_scr[j, :, cs] = _sort_key(jnp.where(vis, acc, -jnp.inf))
            smin_scr[...] = jnp.minimum(smin_scr[...], jnp.where(vis, acc, jnp.inf))
            smax_scr[...] = jnp.maximum(smax_scr[...], jnp.where(vis, acc, -jnp.inf))
        return carry

    lax.fori_loop(0, nvis, score_body, 0)
    smin = jnp.min(smin_scr[...], axis=1, keepdims=True)
    smax = jnp.max(smax_scr[...], axis=1, keepdims=True)

    n_vis = (t0 + lax.broadcasted_iota(jnp.int32, (tq, 1), 0) + 1).astype(F32)
    select_all = n_vis <= topk
    lo0 = jnp.where(select_all, jnp.int32(_KEY_ALL), _sort_key(smin))
    hi0 = jnp.where(select_all, jnp.int32(_KEY_ALL), _sort_key(smax))

    def count_ge(mid):
        mid_b = jnp.broadcast_to(mid, (tq, LANES))

        def body(j, c):
            for cc in range(tk // LANES):
                c = c + jnp.where(key_scr[j, :, cc * LANES:(cc + 1) * LANES] >= mid_b, 1.0, 0.0)
            return c
        c = lax.fori_loop(0, nvis, body, jnp.zeros((tq, LANES), F32))
        return jnp.sum(c, axis=1, keepdims=True)

    def bis_cond(st):
        lo, hi, c_lo = st
        active = (lo < hi) & (c_lo != topk)
        return jnp.max(jnp.where(active, 1.0, 0.0)) > 0.5

    def bis_body(st):
        lo, hi, c_lo = st
        active = (lo < hi) & (c_lo != topk)
        mid_f = 0.5 * _key_to_f32(lo) + 0.5 * _key_to_f32(hi)
        mid = jnp.clip(_sort_key(mid_f), lo + 1, hi)
        mid = jnp.where(active, mid, lo)
        c = count_ge(mid)
        ge = c >= topk
        lo_n = jnp.where(active & ge, mid, lo)
        c_n = jnp.where(active & ge, c, c_lo)
        hi_n = jnp.where(active & jnp.logical_not(ge), mid - 1, hi)
        return lo_n, hi_n, c_n

    thr, _, _ = lax.while_loop(bis_cond, bis_body, (lo0, hi0, n_vis))

    m_scr[...] = jnp.full(m_scr.shape, -1e30, F32)
    acc_scr[...] = jnp.zeros(acc_scr.shape, F32)
    thr_b = jnp.broadcast_to(thr, (tq, tk))

    def attn_body(j, carry):
        off = pl.multiple_of(j * tk, tk)
        bias = jnp.where(key_scr[j] >= thr_b, 0.0, -jnp.inf)
        s = _dot_nt(qb_ref[...].reshape(DSA_HEADS * tq, LANES), kb_ref[pl.ds(off, tk), :])
        s = (s.reshape(DSA_HEADS, tq, tk) + bias[None]).reshape(DSA_HEADS * tq, tk)
        m_old = m_scr[...]
        m_new = jnp.maximum(m_old, jnp.max(s, axis=1, keepdims=True))
        p = jnp.exp(s - jnp.concatenate([m_new] * (tk // LANES), axis=1))
        pv = jnp.dot(p.astype(BF16), va_ref[pl.ds(off, tk), :], preferred_element_type=F32)
        alpha = jnp.exp(m_old - m_new)
        acc_scr[...] = jnp.concatenate([alpha, alpha], axis=1) * acc_scr[...] + pv
        m_scr[...] = m_new
        return carry

    lax.fori_loop(0, nvis, attn_body, 0)
    for h in range(DSA_HEADS):
        rs = slice(h * tq, (h + 1) * tq)
        o_ref[:, h * LANES:(h + 1) * LANES] = (acc_scr[rs, 0:LANES] / acc_scr[rs, LANES:2 * LANES]).astype(o_ref.dtype)


def _dsa(qi, qb, proj, kia, kib, kb, va):
    S = kb.shape[0]
    tq, tk = min(DSA_TQ, S), DSA_TK
    topk = min(DSA_TOPK, S // 4)
    nq = DSA_HEADS * DSA_HEAD_DIM
    full = lambda w: pl.BlockSpec((S, w), lambda i: (0, 0), pipeline_mode=pl.Buffered(1))
    return pl.pallas_call(
        functools.partial(_dsa_kernel, topk),
        out_shape=jax.ShapeDtypeStruct((S, nq), BF16),
        grid=(S // tq,),
        in_specs=[pl.BlockSpec((IDX_HEADS // 2, tq, LANES), lambda i: (0, i, 0)),
                  pl.BlockSpec((DSA_HEADS, tq, LANES), lambda i: (0, i, 0)),
                  pl.BlockSpec((tq, LANES), lambda i: (i, COL_TAIL // LANES)),
                  full(LANES), full(LANES), full(LANES), full(2 * LANES)],
        out_specs=pl.BlockSpec((tq, nq), lambda i: (i, 0)),
        scratch_shapes=[pltpu.VMEM((S // tk, tq, tk), jnp.int32),
                        pltpu.VMEM((IDX_HEADS, tq, LANES), F32),
                        pltpu.VMEM((tq, LANES), F32),
                        pltpu.VMEM((tq, LANES), F32),
                        pltpu.VMEM((DSA_HEADS * tq, LANES), F32),
                        pltpu.VMEM((DSA_HEADS * tq, 2 * LANES), F32)],
        compiler_params=_cparams(("arbitrary",)),
        name="dsa",
    )(qi, qb, proj, kia, kib, kb, va)


def _xattn_kernel(q_ref, kv_ref, o_ref):
    d = XATTN_HEAD_DIM
    nkv = XATTN_HEADS * d
    for h in range(XATTN_HEADS):
        q = q_ref[:, h * d:(h + 1) * d]
        k = kv_ref[:, h * d:(h + 1) * d]
        v = kv_ref[:, nkv + h * d:nkv + (h + 1) * d]
        s = _dot_nt(q, k) * (d ** -0.5)
        s = s - jnp.max(s, axis=1, keepdims=True)
        p = jnp.exp(s)
        p = p / jnp.sum(p, axis=1, keepdims=True)
        o_ref[:, h * d:(h + 1) * d] = jnp.dot(p.astype(BF16), v, preferred_element_type=F32).astype(o_ref.dtype)


def _xattn(q, kv, tm):
    S, N = q.shape
    M = kv.shape[0]
    tm = min(tm, S)
    return pl.pallas_call(
        _xattn_kernel,
        out_shape=jax.ShapeDtypeStruct((S, N), BF16),
        grid=(S // tm,),
        in_specs=[pl.BlockSpec((tm, N), lambda i: (i, 0)),
                  pl.BlockSpec((M, 2 * N), lambda i: (0, 0))],
        out_specs=pl.BlockSpec((tm, N), lambda i: (i, 0)),
        compiler_params=_cparams(("arbitrary",)),
        name="xattn",
    )(q, kv)


PEER_TS = 256


def _top16_distinct(s):
    vals, cnts = [], []
    rem = s
    for _ in range(PEER_TOPK):
        m = jnp.max(rem, axis=0, keepdims=True)
        eq = rem == m
        cnts.append(jnp.sum(jnp.where(eq, 1.0, 0.0), axis=0, keepdims=True))
        vals.append(m)
        rem = jnp.where(eq, -jnp.inf, rem)
    return vals, cnts


def _stack_rows(rows):
    n, T = len(rows), rows[0].shape[1]
    idx = lax.broadcasted_iota(jnp.int32, (n, T), 0)
    out = jnp.broadcast_to(rows[0], (n, T))
    for r in range(1, n):
        out = jnp.where(idx == r, rows[r], out)
    return out


def _peer_select_kernel(q_ref, sk_ref, c_ref, s2_ref, e1_ref, e2_ref):
    T = q_ref.shape[0]
    for h in range(PEER_HEADS):
        s1 = _dot_nt(sk_ref[h, 0], q_ref[:, (2 * h) * PEER_HALF:(2 * h + 1) * PEER_HALF])
        s2 = _dot_nt(sk_ref[h, 1], q_ref[:, (2 * h + 1) * PEER_HALF:(2 * h + 2) * PEER_HALF])
        a_vals, a_cnts = _top16_distinct(s1)
        b_vals, b_cnts = _top16_distinct(s2)
        b_mat = _stack_rows(b_vals)
        bc_mat = _stack_rows(b_cnts)
        cand = jnp.concatenate([a + b_mat for a in a_vals], axis=0)
        wgt = jnp.concatenate([c * bc_mat for c in a_cnts], axis=0)
        rem = cand
        n = jnp.zeros((1, T), F32)
        thr = jnp.full((1, T), -jnp.inf, F32)
        for _ in range(PEER_TOPK):
            m = jnp.max(rem, axis=0, keepdims=True)
            eq = rem == m
            thr = jnp.where(n < PEER_TOPK, m, thr)
            n = n + jnp.sum(jnp.where(eq, wgt, 0.0), axis=0, keepdims=True)
            rem = jnp.where(eq, -jnp.inf, rem)
        vmax = a_vals[0] + b_vals[0]
        keep = cand >= thr
        z = jnp.sum(jnp.where(keep, wgt * jnp.exp(cand - vmax), 0.0), axis=0, keepdims=True)
        c = jnp.full(s1.shape, jnp.inf, F32)
        for r in range(PEER_TOPK):
            rows = slice(r * PEER_TOPK, (r + 1) * PEER_TOPK)
            c_r = jnp.min(jnp.where(keep[rows], b_mat, jnp.inf), axis=0, keepdims=True)
            c = jnp.where(s1 == a_vals[r], c_r, c)
        c_ref[h] = c
        s2_ref[h] = s2
        e1_ref[h] = jnp.exp(s1 - a_vals[0])
        e2_ref[h] = jnp.exp(s2 - b_vals[0]) / z


def _peer_select(q, sub_keys):
    S = q.shape[0]
    ts = min(PEER_TS, S)
    big = jax.ShapeDtypeStruct((PEER_HEADS, PEER_NKEYS, S), F32)
    bspec = pl.BlockSpec((PEER_HEADS, PEER_NKEYS, ts), lambda i: (0, 0, i))
    return pl.pallas_call(
        _peer_select_kernel,
        out_shape=(big, big, big, big),
        grid=(S // ts,),
        in_specs=[pl.BlockSpec((ts, q.shape[1]), lambda i: (i, 0)),
                  pl.BlockSpec(sub_keys.shape, lambda i: (0, 0, 0, 0))],
        out_specs=(bspec, bspec, bspec, bspec),
        compiler_params=_cparams(("arbitrary",)),
        name="peer_select",
    )(q, sub_keys)


PEER_TM = 512
PEER_TE = 512


def _gelu_tanh_scaled(x, g):
    inner = x * (0.7978845608028654 + 0.035677408136300125 * (x * x))
    return (g * (0.5 * x)) * (1.0 + jnp.tanh(inner))


def _peer_dense_kernel(xn_ref, u_ref, v_ref, c_ref, s2_ref, e1_ref, e2_ref, o_ref, wt_scr):
    e = pl.program_id(1)
    te, tm = u_ref.shape[0], xn_ref.shape[0]
    nk = PEER_NKEYS
    nr = te // nk
    c_rows = [[c_ref[h, pl.ds(e * nr + r, 1), :] for h in range(PEER_HEADS)] for r in range(nr)]
    e1_rows = [[e1_ref[h, pl.ds(e * nr + r, 1), :] for h in range(PEER_HEADS)] for r in range(nr)]
    tok_chunk = 2 * LANES
    for tc in range(tm // tok_chunk):
        ts = slice(tc * tok_chunk, (tc + 1) * tok_chunk)
        act = _dot_nt(u_ref[...], xn_ref[ts, :])
        for r in range(nr):
            for tb in range(tok_chunk // LANES):
                ls = slice(tc * tok_chunk + tb * LANES, tc * tok_chunk + (tb + 1) * LANES)
                g = jnp.zeros((nk, LANES), F32)
                for h in range(PEER_HEADS):
                    g = g + (jnp.where(s2_ref[h, :, ls] >= c_rows[r][h][:, ls], e2_ref[h, :, ls], 0.0)
                             * e1_rows[r][h][:, ls])
                w = _gelu_tanh_scaled(act[r * nk:(r + 1) * nk, tb * LANES:(tb + 1) * LANES], g)
                wt_scr[ls, r * nk:(r + 1) * nk] = w.T.astype(BF16)
    col_chunk = 2 * LANES
    for cc in range(o_ref.shape[1] // col_chunk):
        cs = slice(cc * col_chunk, (cc + 1) * col_chunk)
        contrib = jnp.dot(wt_scr[...], v_ref[:, cs], preferred_element_type=F32)
        prev = jnp.where(e == 0, 0.0, o_ref[:, cs])
        o_ref[:, cs] = prev + contrib


def _peer_dense(xn, u, v, c, s2, e1, e2):
    S, D = xn.shape
    E = u.shape[0]
    tm, te = min(PEER_TM, S), PEER_TE
    fac = pl.BlockSpec((PEER_HEADS, PEER_NKEYS, tm), lambda i, e: (0, 0, i))
    return pl.pallas_call(
        _peer_dense_kernel,
        out_shape=jax.ShapeDtypeStruct((S, D), F32),
        grid=(S // tm, E // te),
        in_specs=[pl.BlockSpec((tm, D), lambda i, e: (i, 0)),
                  pl.BlockSpec((te, D), lambda i, e: (e, 0)),
                  pl.BlockSpec((te, D), lambda i, e: (e, 0)),
                  fac, fac, fac, fac],
        out_specs=pl.BlockSpec((tm, D), lambda i, e: (i, 0)),
        scratch_shapes=[pltpu.VMEM((tm, te), BF16)],
        compiler_params=_cparams(("arbitrary", "arbitrary")),
        name="peer_dense",
    )(xn, u, v, c, s2, e1, e2)


def _final_kernel(h_ref, y_ref, g_ref, o_ref):
    x = h_ref[...] + y_ref[...]
    o_ref[...] = x * lax.rsqrt(jnp.mean(x * x, axis=-1, keepdims=True) + RMS_EPS) * g_ref[...]


def _final(h, y, g, tm):
    S, D = h.shape
    tm = min(tm, S)
    spec = pl.BlockSpec((tm, D), lambda i: (i, 0))
    return pl.pallas_call(
        _final_kernel,
        out_shape=jax.ShapeDtypeStruct((S, D), F32),
        grid=(S // tm,),
        in_specs=[spec, spec, pl.BlockSpec((1, D), lambda i: (0, 0))],
        out_specs=spec,
        compiler_params=_cparams(("arbitrary",)),
        name="final",
    )(h, y, g.reshape(1, D))


def _permute_w_in(w):
    c = _SRC
    sl = lambda a, b: w[:, c[a]:c[b]]
    pad = jnp.zeros((w.shape[0], NP_COLS - COL_TAIL - 96), w.dtype)
    cols = [sl("gv", "gr"), sl("gr", "glow"), sl("dq", "dk"), sl("ga", "gb"), sl("gb", "end"),
            sl("gq", "gk"), sl("gk", "gv"), sl("iq", "ik"), sl("dk", "dv"), sl("dv", "iq"),
            sl("ik", "iw"), sl("glow", "dq"), sl("iw", "ga"), pad]
    return jnp.concatenate(cols, axis=1).astype(BF16)


def _layer(h, mem, positions, norm_mix_g, w_in, gla_gate_w2, gla_gate_b, gla_head_norm_g,
           w_proj_gla, w_proj_dsa, w_out, norm_x_g, norm_mem_g, w_xq, w_xk, w_xv, w_xo,
           norm_ffn_g, peer_wq, peer_sub_keys, peer_u, peer_v):
    (proj,) = _norm_mm(h, norm_mix_g, _permute_w_in(w_in), F32, 1024, 512, "in_proj")
    w2p = jnp.zeros((LANES, GLA_HEADS * GLA_DK), F32).at[TAIL_GLOW:TAIL_GLOW + GLA_GATE_RANK].set(gla_gate_w2)
    o_a = _gla(proj, w2p, gla_gate_b, gla_head_norm_g)
    qb, kb, vb, qi, kia, kib = _rope(proj, positions, 512)
    o_b = _dsa(qi, qb, proj, kia, kib, kb, vb)
    merged = _merge(o_a, o_b, w_proj_gla.astype(BF16), w_proj_dsa.astype(BF16), proj, 1024, 512)
    h = _mm_res(merged, w_out.astype(BF16), h, 1024, 512, "out_proj")

    (q_x,) = _norm_mm(h, norm_x_g, w_xq.astype(BF16), BF16, 1024, 512, "xattn_q")
    (kv_x,) = _norm_mm(mem, norm_mem_g, jnp.concatenate([w_xk, w_xv], axis=1).astype(BF16), BF16, 1024, 512,
                       "xattn_kv")
    o_x = _xattn(q_x, kv_x, 512)
    h = _mm_res(o_x, w_xo.astype(BF16), h, 1024, 512, "xattn_o")

    q_p, xn = _norm_mm(h, norm_ffn_g, peer_wq.astype(BF16), BF16, 1024, 512, "peer_q", keep_xn=True)
    c, s2, e1, e2 = _peer_select(q_p, peer_sub_keys.astype(BF16))
    y = _peer_dense(xn, peer_u.astype(BF16), peer_v.astype(BF16), c, s2, e1, e2)
    return h, y


def kernel(x, mem, positions, norm_mix_g, w_in, gla_gate_w2, gla_gate_b, gla_head_norm_g, w_proj_gla, w_proj_dsa, w_out, norm_x_g, norm_mem_g, w_xq, w_xk, w_xv, w_xo, norm_ffn_g, peer_wq, peer_sub_keys, peer_u, peer_v, norm_final_g):
    B, S, D = x.shape
    depth = w_in.shape[0]
    outs = []
    for b in range(B):
        h = x[b]
        y = None
        for l in range(depth):
            if y is not None:
                h = h + y
            h, y = _layer(h, mem[b], positions[b], norm_mix_g[l], w_in[l], gla_gate_w2[l], gla_gate_b[l],
                          gla_head_norm_g[l], w_proj_gla[l], w_proj_dsa[l], w_out[l], norm_x_g[l],
                          norm_mem_g[l], w_xq[l], w_xk[l], w_xv[l], w_xo[l], norm_ffn_g[l], peer_wq[l],
                          peer_sub_keys[l], peer_u[l], peer_v[l])
        outs.append(_final(h, y, norm_final_g, 512))
    return jnp.stack(outs)
```

```python
import functools
import math

import numpy as np
import jax
import jax.numpy as jnp
from jax import lax
from jax.experimental import pallas as pl
from jax.experimental.pallas import tpu as pltpu

F32 = jnp.float32
BF16 = jnp.bfloat16

D_MODEL = 2048
GLA_HEADS = 4
GLA_DK = 256
GLA_DV = 512
GLA_GATE_RANK = 16
GLA_GATE_TEMP = 16.0
DSA_HEADS = 16
DSA_HEAD_DIM = 128
IDX_HEADS = 16
IDX_DIM = 64
DSA_TOPK = 256
XATTN_HEADS = 4
XATTN_HEAD_DIM = D_MODEL // XATTN_HEADS
PEER_HEADS = 8
PEER_NKEYS = 128
PEER_HALF = 128
PEER_TOPK = 16
ROPE_THETA = 10000.0
RMS_EPS = 1e-6

LANES = 128
V7X_VMEM_LIMIT_BYTES = 56 * 1024 * 1024

COL_GV = 0
COL_GR = 2048
COL_DQ = 4096
COL_GA = 6144
COL_GB = 8192
COL_GQ = 10240
COL_GK = 11264
COL_IQ = 12288
COL_DK = 13312
COL_DV = 13440
COL_TAIL = 13568
TAIL_GLOW = 64
TAIL_IW = 80
NP_COLS = 13824

_SRC = dict(gq=0, gk=1024, gv=2048, gr=4096, glow=6144, dq=6160, dk=8208, dv=8336,
            iq=8464, ik=9488, iw=9552, ga=9568, gb=11616, end=13664)


def _cparams(sem, vmem=V7X_VMEM_LIMIT_BYTES):
    return pltpu.CompilerParams(dimension_semantics=sem, vmem_limit_bytes=vmem)


def _dot_nt(a, b):
    return lax.dot_general(a, b, (((1,), (1,)), ((), ())), preferred_element_type=F32)


def _split3(x):
    hi = x.astype(BF16)
    r1 = x - hi.astype(F32)
    mid = r1.astype(BF16)
    lo = (r1 - mid.astype(F32)).astype(BF16)
    return hi, mid, lo


def _norm_mm_kernel(keep_xn, x_ref, g_ref, w_ref, o_ref, *rest):
    a_scr = rest[-1]

    @pl.when(pl.program_id(1) == 0)
    def _():
        x = x_ref[...]
        y = x * lax.rsqrt(jnp.mean(x * x, axis=-1, keepdims=True) + RMS_EPS) * g_ref[...]
        a_scr[...] = y.astype(BF16)
        if keep_xn:
            rest[0][...] = a_scr[...]

    o_ref[...] = jnp.dot(a_scr[...], w_ref[...], preferred_element_type=F32).astype(o_ref.dtype)


def _norm_mm(x, g, w, out_dtype, tm, tn, name, keep_xn=False):
    M, K = x.shape
    N = w.shape[1]
    tm, tn = min(tm, M), min(tn, N)
    out_shape = [jax.ShapeDtypeStruct((M, N), out_dtype)]
    out_specs = [pl.BlockSpec((tm, tn), lambda i, j: (i, j))]
    if keep_xn:
        out_shape.append(jax.ShapeDtypeStruct((M, K), BF16))
        out_specs.append(pl.BlockSpec((tm, K), lambda i, j: (i, 0)))
    return pl.pallas_call(
        functools.partial(_norm_mm_kernel, keep_xn),
        out_shape=tuple(out_shape),
        grid=(M // tm, N // tn),
        in_specs=[pl.BlockSpec((tm, K), lambda i, j: (i, 0)),
                  pl.BlockSpec((1, K), lambda i, j: (0, 0)),
                  pl.BlockSpec((K, tn), lambda i, j: (0, j))],
        out_specs=tuple(out_specs),
        scratch_shapes=[pltpu.VMEM((tm, K), BF16)],
        compiler_params=_cparams(("arbitrary", "arbitrary")),
        name=name,
    )(x, g.reshape(1, K), w)


def _mm_res_kernel(a_ref, w_ref, r_ref, o_ref):
    o_ref[...] = r_ref[...] + jnp.dot(a_ref[...], w_ref[...], preferred_element_type=F32)


def _mm_res(a, w, res, tm, tn, name):
    M, K = a.shape
    N = w.shape[1]
    tm, tn = min(tm, M), min(tn, N)
    return pl.pallas_call(
        _mm_res_kernel,
        out_shape=jax.ShapeDtypeStruct((M, N), F32),
        grid=(M // tm, N // tn),
        in_specs=[pl.BlockSpec((tm, K), lambda i, j: (i, 0)),
                  pl.BlockSpec((K, tn), lambda i, j: (0, j)),
                  pl.BlockSpec((tm, tn), lambda i, j: (i, j))],
        out_specs=pl.BlockSpec((tm, tn), lambda i, j: (i, j)),
        compiler_params=_cparams(("arbitrary", "arbitrary")),
        name=name,
    )(a, w, res)


def _merge_kernel(oa_ref, ob_ref, wa_ref, wb_ref, ga_ref, gb_ref, o_ref):
    ya = jnp.dot(oa_ref[...], wa_ref[...], preferred_element_type=F32)
    yb = jnp.dot(ob_ref[...], wb_ref[...], preferred_element_type=F32)
    o_ref[...] = (jax.nn.sigmoid(ga_ref[...]) * ya + jax.nn.sigmoid(gb_ref[...]) * yb).astype(o_ref.dtype)


def _merge(oa, ob, wa, wb, proj, tm, tn):
    M, K = oa.shape
    N = wa.shape[1]
    tm, tn = min(tm, M), min(tn, N)
    ca, cb = COL_GA // tn, COL_GB // tn
    return pl.pallas_call(
        _merge_kernel,
        out_shape=jax.ShapeDtypeStruct((M, N), BF16),
        grid=(M // tm, N // tn),
        in_specs=[pl.BlockSpec((tm, K), lambda i, j: (i, 0)),
                  pl.BlockSpec((tm, K), lambda i, j: (i, 0)),
                  pl.BlockSpec((K, tn), lambda i, j: (0, j)),
                  pl.BlockSpec((K, tn), lambda i, j: (0, j)),
                  pl.BlockSpec((tm, tn), lambda i, j: (i, ca + j)),
                  pl.BlockSpec((tm, tn), lambda i, j: (i, cb + j))],
        out_specs=pl.BlockSpec((tm, tn), lambda i, j: (i, j)),
        compiler_params=_cparams(("arbitrary", "arbitrary")),
        name="merge",
    )(oa, ob, wa, wb, proj, proj)


GLA_C = 128
GLA_LEVELS = (64, 32, 16)
GLA_DIAG = 16


def _gla_matrices(C):
    t = np.arange(C)[:, None]
    u = np.arange(C)[None, :]
    L = (u <= t).astype(np.float32)
    mats = [L, (u > t).astype(np.float32)]
    for m in GLA_LEVELS:
        anchor = (t // (2 * m)) * (2 * m) + m - 1
        mats.append(L - (u <= anchor).astype(np.float32))
    anchor = (t // GLA_DIAG) * GLA_DIAG
    mats.append(L - (u <= anchor).astype(np.float32))
    return np.concatenate(mats, axis=0)


GLA_HPB = 4


def _gla_kernel(q_ref, k_ref, v_ref, r_ref, tail_ref, w2_ref, gb_ref, hg_ref, mats_ref, o_ref, st_scr):
    C = GLA_C

    @pl.when(pl.program_id(1) == 0)
    def _():
        st_scr[...] = jnp.zeros_like(st_scr)

    t_hi, t_mid, _ = _split3(tail_ref[...])
    mats = mats_ref[...]
    row = lax.broadcasted_iota(jnp.int32, (C, C), 0)
    col = lax.broadcasted_iota(jnp.int32, (C, C), 1)
    masks = []
    for m in GLA_LEVELS:
        sh = int(math.log2(2 * m))
        masks.append(((row >> sh) == (col >> sh)) & ((row & m) != 0) & ((col & m) == 0))
    sh = int(math.log2(GLA_DIAG))
    masks.append(((row >> sh) == (col >> sh)) & (col <= row))

    for hh in range(GLA_HPB):
        ks = slice(hh * GLA_DK, (hh + 1) * GLA_DK)
        vs = slice(hh * GLA_DV, (hh + 1) * GLA_DV)
        w_hi, w_mid, _ = _split3(w2_ref[:, ks])
        z = (jnp.dot(t_hi, w_hi, preferred_element_type=F32) + jnp.dot(t_hi, w_mid, preferred_element_type=F32)
             + jnp.dot(t_mid, w_hi, preferred_element_type=F32)) + gb_ref[:, ks]
        log_a = (jnp.minimum(z, 0.0) - jnp.log1p(jnp.exp(-jnp.abs(z)))) * (1.0 / GLA_GATE_TEMP)

        a_hi, a_mid, a_lo = _split3(log_a)
        dec = (jnp.dot(mats, a_hi, preferred_element_type=F32) + jnp.dot(mats, a_mid, preferred_element_type=F32)
               + jnp.dot(mats, a_lo, preferred_element_type=F32))
        b = dec[0:C]
        b_rest = dec[C:2 * C]

        q = q_ref[:, ks] * (GLA_DK ** -0.5)
        k = k_ref[:, ks]
        v = v_ref[:, vs]
        v_bf = v.astype(BF16)

        attn = jnp.zeros((C, C), F32)
        for li in range(len(GLA_LEVELS) + 1):
            d = dec[(2 + li) * C:(3 + li) * C]
            if li < len(GLA_LEVELS):
                qs = (q * jnp.exp(jnp.minimum(d, 0.0))).astype(BF16)
                ks_ = (k * jnp.exp(jnp.minimum(-d, 0.0))).astype(BF16)
            else:
                qs = (q * jnp.exp(d)).astype(BF16)
                ks_ = (k * jnp.exp(-d)).astype(BF16)
            attn = attn + jnp.where(masks[li], _dot_nt(qs, ks_), 0.0)

        st = st_scr[hh]
        o = _dot_nt((q * jnp.exp(b)).astype(BF16), st.astype(BF16))
        o = o + jnp.dot(attn.astype(BF16), v_bf, preferred_element_type=F32)

        kd = (k * jnp.exp(b_rest)).astype(BF16)
        upd = jnp.dot(v.T.astype(BF16), kd, preferred_element_type=F32)
        st_scr[hh] = st * jnp.exp(b[C - 1:C, :]) + upd

        y = o * lax.rsqrt(jnp.mean(o * o, axis=-1, keepdims=True) + RMS_EPS) * hg_ref[:, vs]
        r = r_ref[:, vs]
        o_ref[:, vs] = (y * (r * jax.nn.sigmoid(r))).astype(o_ref.dtype)


def _gla(proj, w2p, gate_b, head_g):
    S = proj.shape[0]
    C = GLA_C
    G = GLA_HPB
    mats = jnp.asarray(_gla_matrices(C), dtype=BF16)
    nm = mats.shape[0]
    wk, wv = G * GLA_DK, G * GLA_DV
    cq, ck = COL_GQ // wk, COL_GK // wk
    cv, cr = COL_GV // wv, COL_GR // wv
    ct = COL_TAIL // LANES
    return pl.pallas_call(
        _gla_kernel,
        out_shape=jax.ShapeDtypeStruct((S, GLA_HEADS * GLA_DV), BF16),
        grid=(GLA_HEADS // G, S // C),
        in_specs=[pl.BlockSpec((C, wk), lambda h, c: (c, cq + h)),
                  pl.BlockSpec((C, wk), lambda h, c: (c, ck + h)),
                  pl.BlockSpec((C, wv), lambda h, c: (c, cv + h)),
                  pl.BlockSpec((C, wv), lambda h, c: (c, cr + h)),
                  pl.BlockSpec((C, LANES), lambda h, c: (c, ct)),
                  pl.BlockSpec((LANES, wk), lambda h, c: (0, h)),
                  pl.BlockSpec((1, wk), lambda h, c: (0, h)),
                  pl.BlockSpec((1, wv), lambda h, c: (0, h)),
                  pl.BlockSpec((nm, C), lambda h, c: (0, 0))],
        out_specs=pl.BlockSpec((C, wv), lambda h, c: (c, h)),
        scratch_shapes=[pltpu.VMEM((G, GLA_DV, GLA_DK), F32)],
        compiler_params=_cparams(("arbitrary", "arbitrary")),
        name="gla",
    )(proj, proj, proj, proj, proj, w2p, gate_b.reshape(1, -1), head_g.reshape(1, -1), mats)


def _rope_kernel(pos_ref, dq_ref, dkv_ref, iq_ref, tail_ref, inv_ref, sgn_ref,
                 qb_ref, kb_ref, vb_ref, qi_ref, kia_ref, kib_ref):
    pos = pos_ref[...].astype(F32)
    lane = lax.broadcasted_iota(jnp.int32, (pos.shape[0], LANES), 1)

    ang = pos * inv_ref[0:1, :]
    c128 = jnp.cos(ang)
    s128 = jnp.sin(ang) * sgn_ref[0:1, :]
    ang = pos * inv_ref[1:2, :]
    c64 = jnp.cos(ang)
    s64 = jnp.sin(ang) * sgn_ref[1:2, :]

    def rot128(x):
        return x * c128 + pltpu.roll(x, 64, 1) * s128

    def rot64(x):
        swapped = jnp.where(lane % 64 < 32, pltpu.roll(x, 96, 1), pltpu.roll(x, 32, 1))
        return x * c64 + swapped * s64

    scale = (DSA_HEAD_DIM ** -0.5) * math.log2(math.e)
    for h in range(DSA_HEADS):
        qb_ref[h] = (rot128(dq_ref[:, h * LANES:(h + 1) * LANES]) * scale).astype(BF16)
    kb_ref[...] = rot128(dkv_ref[:, 0:LANES]).astype(BF16)
    vb_ref[:, 0:LANES] = dkv_ref[:, LANES:2 * LANES].astype(BF16)
    vb_ref[:, LANES:2 * LANES] = jnp.ones((pos.shape[0], LANES), BF16)
    for h in range(IDX_HEADS // 2):
        qi_ref[h] = rot64(iq_ref[:, h * LANES:(h + 1) * LANES]).astype(BF16)
    ki = jnp.where(lane < IDX_DIM, rot64(tail_ref[...]), 0.0)
    kia_ref[...] = ki.astype(BF16)
    kib_ref[...] = pltpu.roll(ki, 64, 1).astype(BF16)


def _rope(proj, positions, tm):
    S = proj.shape[0]
    tm = min(tm, S)
    inv128 = ROPE_THETA ** (-jnp.arange(0, DSA_HEAD_DIM, 2, dtype=F32) / DSA_HEAD_DIM)
    inv64 = ROPE_THETA ** (-jnp.arange(0, IDX_DIM, 2, dtype=F32) / IDX_DIM)
    inv = jnp.stack([jnp.tile(inv128, 2), jnp.tile(inv64, 4)])
    sgn = jnp.asarray(np.stack([np.repeat([-1.0, 1.0], 64), np.tile(np.repeat([-1.0, 1.0], 32), 2)]), F32)
    nq = DSA_HEADS * DSA_HEAD_DIM
    ni = IDX_HEADS * IDX_DIM
    outs = (jax.ShapeDtypeStruct((DSA_HEADS, S, LANES), BF16), jax.ShapeDtypeStruct((S, LANES), BF16),
            jax.ShapeDtypeStruct((S, 2 * LANES), BF16), jax.ShapeDtypeStruct((IDX_HEADS // 2, S, LANES), BF16),
            jax.ShapeDtypeStruct((S, LANES), BF16), jax.ShapeDtypeStruct((S, LANES), BF16))
    row = lambda w: pl.BlockSpec((tm, w), lambda i: (i, 0))
    heads = lambda n: pl.BlockSpec((n, tm, LANES), lambda i: (0, i, 0))
    return pl.pallas_call(
        _rope_kernel,
        out_shape=outs,
        grid=(S // tm,),
        in_specs=[pl.BlockSpec((tm, 1), lambda i: (i, 0)),
                  pl.BlockSpec((tm, nq), lambda i: (i, COL_DQ // nq)),
                  pl.BlockSpec((tm, 2 * LANES), lambda i: (i, COL_DK // (2 * LANES))),
                  pl.BlockSpec((tm, ni), lambda i: (i, COL_IQ // ni)),
                  pl.BlockSpec((tm, LANES), lambda i: (i, COL_TAIL // LANES)),
                  pl.BlockSpec((2, LANES), lambda i: (0, 0)),
                  pl.BlockSpec((2, LANES), lambda i: (0, 0))],
        out_specs=(heads(DSA_HEADS), row(LANES), row(2 * LANES), heads(IDX_HEADS // 2), row(LANES), row(LANES)),
        compiler_params=_cparams(("arbitrary",)),
        name="rope",
    )(positions.reshape(S, 1), proj, proj, proj, proj, inv, sgn)


DSA_TQ = 256
DSA_TK = 256
_SELECT_ALL = float(np.finfo(np.float32).min)


def _dsa_kernel(topk, qi_ref, qb_ref, tail_ref, kia_ref, kib_ref, kb_ref, va_ref, o_ref,
                sc_scr, wbc_scr, smin_scr, smax_scr, m_scr, acc_scr):
    tq, tk = qb_ref.shape[1], DSA_TK
    nhp = IDX_HEADS // 2
    i = pl.program_id(0)
    t0 = i * tq
    nvis = (t0 + tq - 1) // tk + 1

    w_idx = tail_ref[:, TAIL_IW:TAIL_IW + IDX_HEADS] * ((IDX_DIM ** -0.5) * (IDX_HEADS ** -0.5))
    for h in range(IDX_HEADS):
        wbc_scr[h] = jnp.broadcast_to(w_idx[:, h:h + 1], (tq, LANES))
    smin_scr[...] = jnp.full((tq, LANES), jnp.inf, F32)
    smax_scr[...] = jnp.full((tq, LANES), -jnp.inf, F32)
    t_row = t0 + lax.broadcasted_iota(jnp.int32, (tq, LANES), 0)
    lane = lax.broadcasted_iota(jnp.int32, (tq, LANES), 1)

    def score_body(j, carry):
        off = pl.multiple_of(j * tk, tk)
        qi2 = qi_ref[...].reshape(nhp * tq, LANES)
        s_a = _dot_nt(qi2, kia_ref[pl.ds(off, tk), :])
        s_b = _dot_nt(qi2, kib_ref[pl.ds(off, tk), :])
        for c in range(tk // LANES):
            cs = slice(c * LANES, (c + 1) * LANES)
            acc = jnp.zeros((tq, LANES), F32)
            for hp in range(nhp):
                rs = slice(hp * tq, (hp + 1) * tq)
                acc = acc + jnp.maximum(s_a[rs, cs], 0.0) * wbc_scr[2 * hp]
                acc = acc + jnp.maximum(s_b[rs, cs], 0.0) * wbc_scr[2 * hp + 1]
            vis = (off + c * LANES + lane) <= t_row
            sc_scr[j, :, cs] = jnp.where(vis, acc, -jnp.inf)
            smin_scr[...] = jnp.minimum(smin_scr[...], jnp.where(vis, acc, jnp.inf))
            smax_scr[...] = jnp.maximum(smax_scr[...], jnp.where(vis, acc, -jnp.inf))
        return carry

    lax.fori_loop(0, nvis, score_body, 0)
    smin = jnp.min(smin_scr[...], axis=1, keepdims=True)
    smax = jnp.max(smax_scr[...], axis=1, keepdims=True)

    n_vis = (t0 + lax.broadcasted_iota(jnp.int32, (tq, 1), 0) + 1).astype(F32)
    select_all = n_vis <= topk
    lo0 = jnp.where(select_all, _SELECT_ALL, smin)
    hi0 = jnp.where(select_all, _SELECT_ALL, smax)

    def count_ge(x):
        x_b = jnp.broadcast_to(x, (tq, LANES))

        def body(j, c):
            for cc in range(tk // LANES):
                c = c + jnp.where(sc_scr[j, :, cc * LANES:(cc + 1) * LANES] >= x_b, 1.0, 0.0)
            return c
        c = lax.fori_loop(0, nvis, body, jnp.zeros((tq, LANES), F32))
        return jnp.sum(c, axis=1, keepdims=True)

    def bis_cond(st):
        lo, hi, c_lo = st
        active = (lo < hi) & (c_lo != topk)
        return jnp.max(jnp.where(active, 1.0, 0.0)) > 0.5

    def bis_body(st):
        lo, hi, c_lo = st
        active = (lo < hi) & (c_lo != topk)
        mid = 0.5 * lo + 0.5 * hi
        stuck = (mid <= lo) | (mid >= hi)
        probe = jnp.where(active, jnp.where(stuck, hi, mid), lo)
        c = count_ge(probe)
        ge = c >= topk
        lo_n = jnp.where(active & ge, probe, lo)
        c_n = jnp.where(active & ge, c, c_lo)
        hi_n = jnp.where(active & jnp.logical_not(ge), jnp.where(stuck, lo, mid), hi)
        return lo_n, hi_n, c_n

    thr, _, _ = lax.while_loop(bis_cond, bis_body, (lo0, hi0, n_vis))

    m_scr[...] = jnp.full(m_scr.shape, -1e30, F32)
    acc_scr[...] = jnp.zeros(acc_scr.shape, F32)
    thr_b = jnp.broadcast_to(thr, (tq, tk))

    def attn_body(j, carry):
        off = pl.multiple_of(j * tk, tk)
        bias = jnp.where(sc_scr[j] >= thr_b, 0.0, -jnp.inf)
        s = _dot_nt(qb_ref[...].reshape(DSA_HEADS * tq, LANES), kb_ref[pl.ds(off, tk), :])
        s = (s.reshape(DSA_HEADS, tq, tk) + bias[None]).reshape(DSA_HEADS * tq, tk)
        m_old = m_scr[...]
        m_new = jnp.maximum(m_old, jnp.max(s, axis=1, keepdims=True))
        p = jnp.exp2(s - jnp.concatenate([m_new] * (tk // LANES), axis=1))
        pv = jnp.dot(p.astype(BF16), va_ref[pl.ds(off, tk), :], preferred_element_type=F32)
        alpha = jnp.exp2(m_old - m_new)
        acc_scr[...] = jnp.concatenate([alpha, alpha], axis=1) * acc_scr[...] + pv
        m_scr[...] = m_new
        return carry

    lax.fori_loop(0, nvis, attn_body, 0)
    for h in range(DSA_HEADS):
        rs = slice(h * tq, (h + 1) * tq)
        o_ref[:, h * LANES:(h + 1) * LANES] = (acc_scr[rs, 0:LANES] / acc_scr[rs, LANES:2 * LANES]).astype(o_ref.dtype)


def _dsa(qi, qb, proj, kia, kib, kb, va):
    S = kb.shape[0]
    tq, tk = min(DSA_TQ, S), DSA_TK
    topk = min(DSA_TOPK, S // 4)
    nq = DSA_HEADS * DSA_HEAD_DIM
    full = lambda w: pl.BlockSpec((S, w), lambda i: (0, 0), pipeline_mode=pl.Buffered(1))
    return pl.pallas_call(
        functools.partial(_dsa_kernel, topk),
        out_shape=jax.ShapeDtypeStruct((S, nq), BF16),
        grid=(S // tq,),
        in_specs=[pl.BlockSpec((IDX_HEADS // 2, tq, LANES), lambda i: (0, i, 0)),
                  pl.BlockSpec((DSA_HEADS, tq, LANES), lambda i: (0, i, 0)),
                  pl.BlockSpec((tq, LANES), lambda i: (i, COL_TAIL // LANES)),
                  full(LANES), full(LANES), full(LANES), full(2 * LANES)],
        out_specs=pl.BlockSpec((tq, nq), lambda i: (i, 0)),
        scratch_shapes=[pltpu.VMEM((S // tk, tq, tk), F32),
                        pltpu.VMEM((IDX_HEADS, tq, LANES), F32),
                        pltpu.VMEM((tq, LANES), F32),
                        pltpu.VMEM((tq, LANES), F32),
                        pltpu.VMEM((DSA_HEADS * tq, LANES), F32),
                        pltpu.VMEM((DSA_HEADS * tq, 2 * LANES), F32)],
        compiler_params=_cparams(("arbitrary",)),
        name="dsa",
    )(qi, qb, proj, kia, kib, kb, va)


def _xattn_kernel(q_ref, kv_ref, o_ref):
    d = XATTN_HEAD_DIM
    nkv = XATTN_HEADS * d
    for h in range(XATTN_HEADS):
        q = q_ref[:, h * d:(h + 1) * d]
        k = kv_ref[:, h * d:(h + 1) * d]
        v = kv_ref[:, nkv + h * d:nkv + (h + 1) * d]
        s = _dot_nt(q, k) * (d ** -0.5)
        s = s - jnp.max(s, axis=1, keepdims=True)
        p = jnp.exp(s)
        p = p / jnp.sum(p, axis=1, keepdims=True)
        o_ref[:, h * d:(h + 1) * d] = jnp.dot(p.astype(BF16), v, preferred_element_type=F32).astype(o_ref.dtype)


def _xattn(q, kv, tm):
    S, N = q.shape
    M = kv.shape[0]
    tm = min(tm, S)
    return pl.pallas_call(
        _xattn_kernel,
        out_shape=jax.ShapeDtypeStruct((S, N), BF16),
        grid=(S // tm,),
        in_specs=[pl.BlockSpec((tm, N), lambda i: (i, 0)),
                  pl.BlockSpec((M, 2 * N), lambda i: (0, 0))],
        out_specs=pl.BlockSpec((tm, N), lambda i: (i, 0)),
        compiler_params=_cparams(("arbitrary",)),
        name="xattn",
    )(q, kv)


PEER_TS = 256


def _top16_distinct(s):
    vals, cnts = [], []
    rem = s
    for _ in range(PEER_TOPK):
        m = jnp.max(rem, axis=0, keepdims=True)
        eq = rem == m
        cnts.append(jnp.sum(jnp.where(eq, 1.0, 0.0), axis=0, keepdims=True))
        vals.append(m)
        rem = jnp.where(eq, -jnp.inf, rem)
    return vals, cnts


def _stack_rows(rows):
    n, T = len(rows), rows[0].shape[1]
    idx = lax.broadcasted_iota(jnp.int32, (n, T), 0)
    out = jnp.broadcast_to(rows[0], (n, T))
    for r in range(1, n):
        out = jnp.where(idx == r, rows[r], out)
    return out


def _peer_select_kernel(q_ref, sk_ref, c_ref, s2_ref, e1_ref, e2_ref):
    T = q_ref.shape[0]
    for h in range(PEER_HEADS):
        s1 = _dot_nt(sk_ref[h, 0], q_ref[:, (2 * h) * PEER_HALF:(2 * h + 1) * PEER_HALF])
        s2 = _dot_nt(sk_ref[h, 1], q_ref[:, (2 * h + 1) * PEER_HALF:(2 * h + 2) * PEER_HALF])
        a_vals, a_cnts = _top16_distinct(s1)
        b_vals, b_cnts = _top16_distinct(s2)
        b_mat = _stack_rows(b_vals)
        bc_mat = _stack_rows(b_cnts)
        cand = jnp.concatenate([a + b_mat for a in a_vals], axis=0)
        wgt = jnp.concatenate([c * bc_mat for c in a_cnts], axis=0)
        rem = cand
        n = jnp.zeros((1, T), F32)
        thr = jnp.full((1, T), -jnp.inf, F32)
        for _ in range(PEER_TOPK):
            m = jnp.max(rem, axis=0, keepdims=True)
            eq = rem == m
            thr = jnp.where(n < PEER_TOPK, m, thr)
            n = n + jnp.sum(jnp.where(eq, wgt, 0.0), axis=0, keepdims=True)
            rem = jnp.where(eq, -jnp.inf, rem)
        vmax = a_vals[0] + b_vals[0]
        keep = cand >= thr
        z = jnp.sum(jnp.where(keep, wgt * jnp.exp(cand - vmax), 0.0), axis=0, keepdims=True)
        c = jnp.full(s1.shape, jnp.inf, F32)
        for r in range(PEER_TOPK):
            rows = slice(r * PEER_TOPK, (r + 1) * PEER_TOPK)
            c_r = jnp.min(jnp.where(keep[rows], b_mat, jnp.inf), axis=0, keepdims=True)
            c = jnp.where(s1 == a_vals[r], c_r, c)
        c_ref[h] = c
        s2_ref[h] = s2
        e1_ref[h] = jnp.exp(s1 - a_vals[0])
        e2_ref[h] = jnp.exp(s2 - b_vals[0]) / z


def _peer_select(q, sub_keys):
    S = q.shape[0]
    ts = min(PEER_TS, S)
    big = jax.ShapeDtypeStruct((PEER_HEADS, PEER_NKEYS, S), F32)
    bspec = pl.BlockSpec((PEER_HEADS, PEER_NKEYS, ts), lambda i: (0, 0, i))
    return pl.pallas_call(
        _peer_select_kernel,
        out_shape=(big, big, big, big),
        grid=(S // ts,),
        in_specs=[pl.BlockSpec((ts, q.shape[1]), lambda i: (i, 0)),
                  pl.BlockSpec(sub_keys.shape, lambda i: (0, 0, 0, 0))],
        out_specs=(bspec, bspec, bspec, bspec),
        compiler_params=_cparams(("arbitrary",)),
        name="peer_select",
    )(q, sub_keys)


PEER_TM = 512
PEER_TE = 512


def _gelu_tanh_scaled(x, g):
    inner = x * (0.7978845608028654 + 0.035677408136300125 * (x * x))
    return (g * (0.5 * x)) * (1.0 + jnp.tanh(inner))


def _peer_dense_kernel(xn_ref, u_ref, v_ref, c_ref, s2_ref, e1_ref, e2_ref, o_ref, wt_scr):
    e = pl.program_id(1)
    te, tm = u_ref.shape[0], xn_ref.shape[0]
    nk = PEER_NKEYS
    nr = te // nk
    c_rows = [[c_ref[h, pl.ds(e * nr + r, 1), :] for h in range(PEER_HEADS)] for r in range(nr)]
    e1_rows = [[e1_ref[h, pl.ds(e * nr + r, 1), :] for h in range(PEER_HEADS)] for r in range(nr)]
    tok_chunk = 2 * LANES
    for tc in range(tm // tok_chunk):
        ts = slice(tc * tok_chunk, (tc + 1) * tok_chunk)
        act = _dot_nt(u_ref[...], xn_ref[ts, :])
        for r in range(nr):
            for tb in range(tok_chunk // LANES):
                ls = slice(tc * tok_chunk + tb * LANES, tc * tok_chunk + (tb + 1) * LANES)
                g = jnp.zeros((nk, LANES), F32)
                for h in range(PEER_HEADS):
                    g = g + (jnp.where(s2_ref[h, :, ls] >= c_rows[r][h][:, ls], e2_ref[h, :, ls], 0.0)
                             * e1_rows[r][h][:, ls])
                w = _gelu_tanh_scaled(act[r * nk:(r + 1) * nk, tb * LANES:(tb + 1) * LANES], g)
                wt_scr[ls, r * nk:(r + 1) * nk] = w.T.astype(BF16)
    col_chunk = 2 * LANES
    for cc in range(o_ref.shape[1] // col_chunk):
        cs = slice(cc * col_chunk, (cc + 1) * col_chunk)
        contrib = jnp.dot(wt_scr[...], v_ref[:, cs], preferred_element_type=F32)
        prev = jnp.where(e == 0, 0.0, o_ref[:, cs])
        o_ref[:, cs] = prev + contrib


def _peer_dense(xn, u, v, c, s2, e1, e2):
    S, D = xn.shape
    E = u.shape[0]
    tm, te = min(PEER_TM, S), PEER_TE
    fac = pl.BlockSpec((PEER_HEADS, PEER_NKEYS, tm), lambda i, e: (0, 0, i))
    return pl.pallas_call(
        _peer_dense_kernel,
        out_shape=jax.ShapeDtypeStruct((S, D), F32),
        grid=(S // tm, E // te),
        in_specs=[pl.BlockSpec((tm, D), lambda i, e: (i, 0)),
                  pl.BlockSpec((te, D), lambda i, e: (e, 0)),
                  pl.BlockSpec((te, D), lambda i, e: (e, 0)),
                  fac, fac, fac, fac],
        out_specs=pl.BlockSpec((tm, D), lambda i, e: (i, 0)),
        scratch_shapes=[pltpu.VMEM((tm, te), BF16)],
        compiler_params=_cparams(("arbitrary", "arbitrary")),
        name="peer_dense",
    )(xn, u, v, c, s2, e1, e2)


def _final_kernel(h_ref, y_ref, g_ref, o_ref):
    x = h_ref[...] + y_ref[...]
    o_ref[...] = x * lax.rsqrt(jnp.mean(x * x, axis=-1, keepdims=True) + RMS_EPS) * g_ref[...]


def _final(h, y, g, tm):
    S, D = h.shape
    tm = min(tm, S)
    spec = pl.BlockSpec((tm, D), lambda i: (i, 0))
    return pl.pallas_call(
        _final_kernel,
        out_shape=jax.ShapeDtypeStruct((S, D), F32),
        grid=(S // tm,),
        in_specs=[spec, spec, pl.BlockSpec((1, D), lambda i: (0, 0))],
        out_specs=spec,
        compiler_params=_cparams(("arbitrary",)),
        name="final",
    )(h, y, g.reshape(1, D))


def _permute_w_in(w_in, l):
    c = _SRC
    sl = lambda a, b: w_in[l, :, c[a]:c[b]]
    pad = jnp.zeros((w_in.shape[1], NP_COLS - COL_TAIL - 96), w_in.dtype)
    cols = [sl("gv", "gr"), sl("gr", "glow"), sl("dq", "dk"), sl("ga", "gb"), sl("gb", "end"),
            sl("gq", "gk"), sl("gk", "gv"), sl("iq", "ik"), sl("dk", "dv"), sl("dv", "iq"),
            sl("ik", "iw"), sl("glow", "dq"), sl("iw", "ga"), pad]
    return jnp.concatenate(cols, axis=1).astype(BF16)


def _layer(h, mem, positions, norm_mix_g, w_in_perm, gla_gate_w2, gla_gate_b, gla_head_norm_g,
           w_proj_gla, w_proj_dsa, w_out, norm_x_g, norm_mem_g, w_xq, w_xk, w_xv, w_xo,
           norm_ffn_g, peer_wq, peer_sub_keys, peer_u, peer_v):
    (proj,) = _norm_mm(h, norm_mix_g, w_in_perm, F32, 1024, 512, "in_proj")
    w2p = jnp.zeros((LANES, GLA_HEADS * GLA_DK), F32).at[TAIL_GLOW:TAIL_GLOW + GLA_GATE_RANK].set(gla_gate_w2)
    o_a = _gla(proj, w2p, gla_gate_b, gla_head_norm_g)
    qb, kb, vb, qi, kia, kib = _rope(proj, positions, 512)
    o_b = _dsa(qi, qb, proj, kia, kib, kb, vb)
    merged = _merge(o_a, o_b, w_proj_gla.astype(BF16), w_proj_dsa.astype(BF16), proj, 1024, 512)
    h = _mm_res(merged, w_out.astype(BF16), h, 1024, 512, "out_proj")

    (q_x,) = _norm_mm(h, norm_x_g, w_xq.astype(BF16), BF16, 1024, 512, "xattn_q")
    (kv_x,) = _norm_mm(mem, norm_mem_g, jnp.concatenate([w_xk, w_xv], axis=1).astype(BF16), BF16, 1024, 512,
                       "xattn_kv")
    o_x = _xattn(q_x, kv_x, 512)
    h = _mm_res(o_x, w_xo.astype(BF16), h, 1024, 512, "xattn_o")

    q_p, xn = _norm_mm(h, norm_ffn_g, peer_wq.astype(BF16), BF16, 1024, 512, "peer_q", keep_xn=True)
    c, s2, e1, e2 = _peer_select(q_p, peer_sub_keys.astype(BF16))
    y = _peer_dense(xn, peer_u.astype(BF16), peer_v.astype(BF16), c, s2, e1, e2)
    return h, y


def kernel(x, mem, positions, norm_mix_g, w_in, gla_gate_w2, gla_gate_b, gla_head_norm_g, w_proj_gla, w_proj_dsa, w_out, norm_x_g, norm_mem_g, w_xq, w_xk, w_xv, w_xo, norm_ffn_g, peer_wq, peer_sub_keys, peer_u, peer_v, norm_final_g):
    B, S, D = x.shape
    depth = w_in.shape[0]
    outs = []
    for b in range(B):
        h = x[b]
        y = None
        for l in range(depth):
            if y is not None:
                h = h + y
            h, y = _layer(h, mem[b], positions[b], norm_mix_g[l], _permute_w_in(w_in, l), gla_gate_w2[l], gla_gate_b[l],
                          gla_head_norm_g[l], w_proj_gla[l], w_proj_dsa[l], w_out[l], norm_x_g[l],
                          norm_mem_g[l], w_xq[l], w_xk[l], w_xv[l], w_xo[l], norm_ffn_g[l], peer_wq[l],
                          peer_sub_keys[l], peer_u[l], peer_v[l])
        outs.append(_final(h, y, norm_final_g, 512))
    return jnp.stack(outs)
```

```python
import functools
import math

import numpy as np
import jax
import jax.numpy as jnp
from jax import lax
from jax.experimental import pallas as pl
from jax.experimental.pallas import tpu as pltpu

F32 = jnp.float32
BF16 = jnp.bfloat16

D_MODEL = 2048
GLA_HEADS = 4
GLA_DK = 256
GLA_DV = 512
GLA_GATE_RANK = 16
GLA_GATE_TEMP = 16.0
DSA_HEADS = 16
DSA_HEAD_DIM = 128
IDX_HEADS = 16
IDX_DIM = 64
DSA_TOPK = 256
XATTN_HEADS = 4
XATTN_HEAD_DIM = D_MODEL // XATTN_HEADS
PEER_HEADS = 8
PEER_NKEYS = 128
PEER_HALF = 128
PEER_TOPK = 16
ROPE_THETA = 10000.0
RMS_EPS = 1e-6

LANES = 128
V7X_VMEM_LIMIT_BYTES = 56 * 1024 * 1024

COL_GV = 0
COL_GR = 2048
COL_DQ = 4096
COL_GA = 6144
COL_GB = 8192
COL_GQ = 10240
COL_GK = 11264
COL_IQ = 12288
COL_DK = 13312
COL_DV = 13440
COL_TAIL = 13568
TAIL_GLOW = 64
TAIL_IW = 80
NP_COLS = 13824

_SRC = dict(gq=0, gk=1024, gv=2048, gr=4096, glow=6144, dq=6160, dk=8208, dv=8336,
            iq=8464, ik=9488, iw=9552, ga=9568, gb=11616, end=13664)


def _cparams(sem, vmem=V7X_VMEM_LIMIT_BYTES):
    return pltpu.CompilerParams(dimension_semantics=sem, vmem_limit_bytes=vmem)


def _weight_spec(shape, index_map, resident):
    if resident:
        return pl.BlockSpec(shape, index_map, pipeline_mode=pl.Buffered(1))
    return pl.BlockSpec(shape, index_map)


def _dot_nt(a, b):
    return lax.dot_general(a, b, (((1,), (1,)), ((), ())), preferred_element_type=F32)


def _split3(x):
    hi = x.astype(BF16)
    r1 = x - hi.astype(F32)
    mid = r1.astype(BF16)
    lo = (r1 - mid.astype(F32)).astype(BF16)
    return hi, mid, lo


def _norm_mm_kernel(keep_xn, x_ref, g_ref, w_ref, o_ref, *rest):
    a_scr = rest[-1]

    @pl.when(pl.program_id(1) == 0)
    def _():
        x = x_ref[...]
        y = x * lax.rsqrt(jnp.mean(x * x, axis=-1, keepdims=True) + RMS_EPS) * g_ref[...]
        a_scr[...] = y.astype(BF16)
        if keep_xn:
            rest[0][...] = a_scr[...]

    o_ref[...] = jnp.dot(a_scr[...], w_ref[...], preferred_element_type=F32).astype(o_ref.dtype)


def _norm_mm(x, g, w, out_dtype, tm, tn, name, keep_xn=False):
    M, K = x.shape
    N = w.shape[1]
    tm, tn = min(tm, M), min(tn, N)
    out_shape = [jax.ShapeDtypeStruct((M, N), out_dtype)]
    out_specs = [pl.BlockSpec((tm, tn), lambda i, j: (i, j))]
    if keep_xn:
        out_shape.append(jax.ShapeDtypeStruct((M, K), BF16))
        out_specs.append(pl.BlockSpec((tm, K), lambda i, j: (i, 0)))
    return pl.pallas_call(
        functools.partial(_norm_mm_kernel, keep_xn),
        out_shape=tuple(out_shape),
        grid=(M // tm, N // tn),
        in_specs=[pl.BlockSpec((tm, K), lambda i, j: (i, 0)),
                  pl.BlockSpec((1, K), lambda i, j: (0, 0)),
                  _weight_spec((K, tn), lambda i, j: (0, j), tn == N)],
        out_specs=tuple(out_specs),
        scratch_shapes=[pltpu.VMEM((tm, K), BF16)],
        compiler_params=_cparams(("arbitrary", "arbitrary")),
        name=name,
    )(x, g.reshape(1, K), w)


def _mm_res_kernel(a_ref, w_ref, r_ref, o_ref):
    o_ref[...] = r_ref[...] + jnp.dot(a_ref[...], w_ref[...], preferred_element_type=F32)


def _mm_res(a, w, res, tm, tn, name):
    M, K = a.shape
    N = w.shape[1]
    tm, tn = min(tm, M), min(tn, N)
    return pl.pallas_call(
        _mm_res_kernel,
        out_shape=jax.ShapeDtypeStruct((M, N), F32),
        grid=(M // tm, N // tn),
        in_specs=[pl.BlockSpec((tm, K), lambda i, j: (i, 0)),
                  _weight_spec((K, tn), lambda i, j: (0, j), tn == N),
                  pl.BlockSpec((tm, tn), lambda i, j: (i, j))],
        out_specs=pl.BlockSpec((tm, tn), lambda i, j: (i, j)),
        compiler_params=_cparams(("arbitrary", "arbitrary")),
        name=name,
    )(a, w, res)


def _merge_kernel(oa_ref, ob_ref, wa_ref, wb_ref, ga_ref, gb_ref, o_ref):
    ya = jnp.dot(oa_ref[...], wa_ref[...], preferred_element_type=F32)
    yb = jnp.dot(ob_ref[...], wb_ref[...], preferred_element_type=F32)
    o_ref[...] = (jax.nn.sigmoid(ga_ref[...]) * ya + jax.nn.sigmoid(gb_ref[...]) * yb).astype(o_ref.dtype)


def _merge(oa, ob, wa, wb, proj, tm, tn):
    M, K = oa.shape
    N = wa.shape[1]
    tm, tn = min(tm, M), min(tn, N)
    ca, cb = COL_GA // tn, COL_GB // tn
    return pl.pallas_call(
        _merge_kernel,
        out_shape=jax.ShapeDtypeStruct((M, N), BF16),
        grid=(M // tm, N // tn),
        in_specs=[pl.BlockSpec((tm, K), lambda i, j: (i, 0)),
                  pl.BlockSpec((tm, K), lambda i, j: (i, 0)),
                  _weight_spec((K, tn), lambda i, j: (0, j), tn == N),
                  _weight_spec((K, tn), lambda i, j: (0, j), tn == N),
                  pl.BlockSpec((tm, tn), lambda i, j: (i, ca + j)),
                  pl.BlockSpec((tm, tn), lambda i, j: (i, cb + j))],
        out_specs=pl.BlockSpec((tm, tn), lambda i, j: (i, j)),
        compiler_params=_cparams(("arbitrary", "arbitrary")),
        name="merge",
    )(oa, ob, wa, wb, proj, proj)


GLA_C = 128
GLA_LEVELS = (64, 32, 16)
GLA_DIAG = 16


def _gla_matrices(C):
    t = np.arange(C)[:, None]
    u = np.arange(C)[None, :]
    L = (u <= t).astype(np.float32)
    mats = [L, (u > t).astype(np.float32)]
    for m in GLA_LEVELS:
        anchor = (t // (2 * m)) * (2 * m) + m - 1
        mats.append(L - (u <= anchor).astype(np.float32))
    anchor = (t // GLA_DIAG) * GLA_DIAG
    mats.append(L - (u <= anchor).astype(np.float32))
    return np.concatenate(mats, axis=0)


GLA_HPB = 4


def _gla_kernel(q_ref, k_ref, v_ref, r_ref, tail_ref, w2_ref, gb_ref, hg_ref, mats_ref, o_ref, st_scr):
    C = GLA_C

    @pl.when(pl.program_id(1) == 0)
    def _():
        st_scr[...] = jnp.zeros_like(st_scr)

    t_hi, t_mid, _ = _split3(tail_ref[...])
    mats = mats_ref[...]
    row = lax.broadcasted_iota(jnp.int32, (C, C), 0)
    col = lax.broadcasted_iota(jnp.int32, (C, C), 1)
    masks = []
    for m in GLA_LEVELS:
        sh = int(math.log2(2 * m))
        masks.append(((row >> sh) == (col >> sh)) & ((row & m) != 0) & ((col & m) == 0))
    sh = int(math.log2(GLA_DIAG))
    masks.append(((row >> sh) == (col >> sh)) & (col <= row))

    for hh in range(GLA_HPB):
        ks = slice(hh * GLA_DK, (hh + 1) * GLA_DK)
        vs = slice(hh * GLA_DV, (hh + 1) * GLA_DV)
        w_hi, w_mid, _ = _split3(w2_ref[:, ks])
        z = (jnp.dot(t_hi, w_hi, preferred_element_type=F32) + jnp.dot(t_hi, w_mid, preferred_element_type=F32)
             + jnp.dot(t_mid, w_hi, preferred_element_type=F32)) + gb_ref[:, ks]
        log_a = (jnp.minimum(z, 0.0) - jnp.log1p(jnp.exp(-jnp.abs(z)))) * (1.0 / GLA_GATE_TEMP)

        a_hi, a_mid, a_lo = _split3(log_a)
        dec = (jnp.dot(mats, a_hi, preferred_element_type=F32) + jnp.dot(mats, a_mid, preferred_element_type=F32)
               + jnp.dot(mats, a_lo, preferred_element_type=F32))
        b = dec[0:C]
        b_rest = dec[C:2 * C]

        q = q_ref[:, ks] * (GLA_DK ** -0.5)
        k = k_ref[:, ks]
        v = v_ref[:, vs]
        v_bf = v.astype(BF16)

        attn = jnp.zeros((C, C), F32)
        for li in range(len(GLA_LEVELS) + 1):
            d = dec[(2 + li) * C:(3 + li) * C]
            if li < len(GLA_LEVELS):
                qs = (q * jnp.exp(jnp.minimum(d, 0.0))).astype(BF16)
                ks_ = (k * jnp.exp(jnp.minimum(-d, 0.0))).astype(BF16)
            else:
                qs = (q * jnp.exp(d)).astype(BF16)
                ks_ = (k * jnp.exp(-d)).astype(BF16)
            attn = attn + jnp.where(masks[li], _dot_nt(qs, ks_), 0.0)

        st = st_scr[hh]
        o = _dot_nt((q * jnp.exp(b)).astype(BF16), st.astype(BF16))
        o = o + jnp.dot(attn.astype(BF16), v_bf, preferred_element_type=F32)

        kd = (k * jnp.exp(b_rest)).astype(BF16)
        upd = jnp.dot(v.T.astype(BF16), kd, preferred_element_type=F32)
        st_scr[hh] = st * jnp.exp(b[C - 1:C, :]) + upd

        y = o * lax.rsqrt(jnp.mean(o * o, axis=-1, keepdims=True) + RMS_EPS) * hg_ref[:, vs]
        r = r_ref[:, vs]
        o_ref[:, vs] = (y * (r * jax.nn.sigmoid(r))).astype(o_ref.dtype)


def _gla(proj, w2p, gate_b, head_g):
    S = proj.shape[0]
    C = GLA_C
    G = GLA_HPB
    mats = jnp.asarray(_gla_matrices(C), dtype=BF16)
    nm = mats.shape[0]
    wk, wv = G * GLA_DK, G * GLA_DV
    cq, ck = COL_GQ // wk, COL_GK // wk
    cv, cr = COL_GV // wv, COL_GR // wv
    ct = COL_TAIL // LANES
    return pl.pallas_call(
        _gla_kernel,
        out_shape=jax.ShapeDtypeStruct((S, GLA_HEADS * GLA_DV), BF16),
        grid=(GLA_HEADS // G, S // C),
        in_specs=[pl.BlockSpec((C, wk), lambda h, c: (c, cq + h)),
                  pl.BlockSpec((C, wk), lambda h, c: (c, ck + h)),
                  pl.BlockSpec((C, wv), lambda h, c: (c, cv + h)),
                  pl.BlockSpec((C, wv), lambda h, c: (c, cr + h)),
                  pl.BlockSpec((C, LANES), lambda h, c: (c, ct)),
                  pl.BlockSpec((LANES, wk), lambda h, c: (0, h)),
                  pl.BlockSpec((1, wk), lambda h, c: (0, h)),
                  pl.BlockSpec((1, wv), lambda h, c: (0, h)),
                  pl.BlockSpec((nm, C), lambda h, c: (0, 0))],
        out_specs=pl.BlockSpec((C, wv), lambda h, c: (c, h)),
        scratch_shapes=[pltpu.VMEM((G, GLA_DV, GLA_DK), F32)],
        compiler_params=_cparams(("arbitrary", "arbitrary")),
        name="gla",
    )(proj, proj, proj, proj, proj, w2p, gate_b.reshape(1, -1), head_g.reshape(1, -1), mats)


def _rope_kernel(pos_ref, dq_ref, dkv_ref, iq_ref, tail_ref, inv_ref, sgn_ref,
                 qb_ref, kb_ref, vb_ref, qi_ref, kia_ref, kib_ref):
    pos = pos_ref[...].astype(F32)
    lane = lax.broadcasted_iota(jnp.int32, (pos.shape[0], LANES), 1)

    ang = pos * inv_ref[0:1, :]
    c128 = jnp.cos(ang)
    s128 = jnp.sin(ang) * sgn_ref[0:1, :]
    ang = pos * inv_ref[1:2, :]
    c64 = jnp.cos(ang)
    s64 = jnp.sin(ang) * sgn_ref[1:2, :]

    def rot128(x):
        return x * c128 + pltpu.roll(x, 64, 1) * s128

    def rot64(x):
        swapped = jnp.where(lane % 64 < 32, pltpu.roll(x, 96, 1), pltpu.roll(x, 32, 1))
        return x * c64 + swapped * s64

    scale = (DSA_HEAD_DIM ** -0.5) * math.log2(math.e)
    for h in range(DSA_HEADS):
        qb_ref[h] = (rot128(dq_ref[:, h * LANES:(h + 1) * LANES]) * scale).astype(BF16)
    kb_ref[...] = rot128(dkv_ref[:, 0:LANES]).astype(BF16)
    vb_ref[:, 0:LANES] = dkv_ref[:, LANES:2 * LANES].astype(BF16)
    vb_ref[:, LANES:2 * LANES] = jnp.ones((pos.shape[0], LANES), BF16)
    for h in range(IDX_HEADS // 2):
        qi_ref[h] = rot64(iq_ref[:, h * LANES:(h + 1) * LANES]).astype(BF16)
    ki = jnp.where(lane < IDX_DIM, rot64(tail_ref[...]), 0.0)
    kia_ref[...] = ki.astype(BF16)
    kib_ref[...] = pltpu.roll(ki, 64, 1).astype(BF16)


def _rope(proj, positions, tm):
    S = proj.shape[0]
    tm = min(tm, S)
    inv128 = ROPE_THETA ** (-jnp.arange(0, DSA_HEAD_DIM, 2, dtype=F32) / DSA_HEAD_DIM)
    inv64 = ROPE_THETA ** (-jnp.arange(0, IDX_DIM, 2, dtype=F32) / IDX_DIM)
    inv = jnp.stack([jnp.tile(inv128, 2), jnp.tile(inv64, 4)])
    sgn = jnp.asarray(np.stack([np.repeat([-1.0, 1.0], 64), np.tile(np.repeat([-1.0, 1.0], 32), 2)]), F32)
    nq = DSA_HEADS * DSA_HEAD_DIM
    ni = IDX_HEADS * IDX_DIM
    outs = (jax.ShapeDtypeStruct((DSA_HEADS, S, LANES), BF16), jax.ShapeDtypeStruct((S, LANES), BF16),
            jax.ShapeDtypeStruct((S, 2 * LANES), BF16), jax.ShapeDtypeStruct((IDX_HEADS // 2, S, LANES), BF16),
            jax.ShapeDtypeStruct((S, LANES), BF16), jax.ShapeDtypeStruct((S, LANES), BF16))
    row = lambda w: pl.BlockSpec((tm, w), lambda i: (i, 0))
    heads = lambda n: pl.BlockSpec((n, tm, LANES), lambda i: (0, i, 0))
    return pl.pallas_call(
        _rope_kernel,
        out_shape=outs,
        grid=(S // tm,),
        in_specs=[pl.BlockSpec((tm, 1), lambda i: (i, 0)),
                  pl.BlockSpec((tm, nq), lambda i: (i, COL_DQ // nq)),
                  pl.BlockSpec((tm, 2 * LANES), lambda i: (i, COL_DK // (2 * LANES))),
                  pl.BlockSpec((tm, ni), lambda i: (i, COL_IQ // ni)),
                  pl.BlockSpec((tm, LANES), lambda i: (i, COL_TAIL // LANES)),
                  pl.BlockSpec((2, LANES), lambda i: (0, 0)),
                  pl.BlockSpec((2, LANES), lambda i: (0, 0))],
        out_specs=(heads(DSA_HEADS), row(LANES), row(2 * LANES), heads(IDX_HEADS // 2), row(LANES), row(LANES)),
        compiler_params=_cparams(("arbitrary",)),
        name="rope",
    )(positions.reshape(S, 1), proj, proj, proj, proj, inv, sgn)


DSA_TQ = 256
DSA_TK = 256
_SELECT_ALL = float(np.finfo(np.float32).min)


def _dsa_kernel(topk, qi_ref, qb_ref, tail_ref, kia_ref, kib_ref, kb_ref, va_ref, o_ref,
                sc_scr, wbc_scr, smin_scr, smax_scr, m_scr, acc_scr):
    tq, tk = qb_ref.shape[1], DSA_TK
    nhp = IDX_HEADS // 2
    i = pl.program_id(0)
    t0 = i * tq
    nvis = (t0 + tq - 1) // tk + 1

    w_idx = tail_ref[:, TAIL_IW:TAIL_IW + IDX_HEADS] * ((IDX_DIM ** -0.5) * (IDX_HEADS ** -0.5))
    for h in range(IDX_HEADS):
        wbc_scr[h] = jnp.broadcast_to(w_idx[:, h:h + 1], (tq, LANES))
    smin_scr[...] = jnp.full((tq, LANES), jnp.inf, F32)
    smax_scr[...] = jnp.full((tq, LANES), -jnp.inf, F32)
    t_row = t0 + lax.broadcasted_iota(jnp.int32, (tq, LANES), 0)
    lane = lax.broadcasted_iota(jnp.int32, (tq, LANES), 1)

    def score_body(j, carry):
        off = pl.multiple_of(j * tk, tk)
        qi2 = qi_ref[...].reshape(nhp * tq, LANES)
        s_a = _dot_nt(qi2, kia_ref[pl.ds(off, tk), :])
        s_b = _dot_nt(qi2, kib_ref[pl.ds(off, tk), :])
        for c in range(tk // LANES):
            cs = slice(c * LANES, (c + 1) * LANES)
            acc = jnp.zeros((tq, LANES), F32)
            for hp in range(nhp):
                rs = slice(hp * tq, (hp + 1) * tq)
                acc = acc + jnp.maximum(s_a[rs, cs], 0.0) * wbc_scr[2 * hp]
                acc = acc + jnp.maximum(s_b[rs, cs], 0.0) * wbc_scr[2 * hp + 1]
            vis = (off + c * LANES + lane) <= t_row
            sc_scr[j, :, cs] = jnp.where(vis, acc, -jnp.inf)
            smin_scr[...] = jnp.minimum(smin_scr[...], jnp.where(vis, acc, jnp.inf))
            smax_scr[...] = jnp.maximum(smax_scr[...], jnp.where(vis, acc, -jnp.inf))
        return carry

    lax.fori_loop(0, nvis, score_body, 0)
    smin = jnp.min(smin_scr[...], axis=1, keepdims=True)
    smax = jnp.max(smax_scr[...], axis=1, keepdims=True)

    n_vis = (t0 + lax.broadcasted_iota(jnp.int32, (tq, 1), 0) + 1).astype(F32)
    select_all = n_vis <= topk
    lo0 = jnp.where(select_all, _SELECT_ALL, smin)
    hi0 = jnp.where(select_all, _SELECT_ALL, smax)

    def count_ge(x):
        x_b = jnp.broadcast_to(x, (tq, LANES))

        def body(j, c):
            for cc in range(tk // LANES):
                c = c + jnp.where(sc_scr[j, :, cc * LANES:(cc + 1) * LANES] >= x_b, 1.0, 0.0)
            return c
        c = lax.fori_loop(0, nvis, body, jnp.zeros((tq, LANES), F32))
        return jnp.sum(c, axis=1, keepdims=True)

    def bis_cond(st):
        lo, hi, c_lo = st
        active = (lo < hi) & (c_lo != topk)
        return jnp.max(jnp.where(active, 1.0, 0.0)) > 0.5

    def bis_body(st):
        lo, hi, c_lo = st
        active = (lo < hi) & (c_lo != topk)
        mid = 0.5 * lo + 0.5 * hi
        stuck = (mid <= lo) | (mid >= hi)
        probe = jnp.where(active, jnp.where(stuck, hi, mid), lo)
        c = count_ge(probe)
        ge = c >= topk
        lo_n = jnp.where(active & ge, probe, lo)
        c_n = jnp.where(active & ge, c, c_lo)
        hi_n = jnp.where(active & jnp.logical_not(ge), jnp.where(stuck, lo, mid), hi)
        return lo_n, hi_n, c_n

    thr, _, _ = lax.while_loop(bis_cond, bis_body, (lo0, hi0, n_vis))

    m_scr[...] = jnp.full(m_scr.shape, -1e30, F32)
    acc_scr[...] = jnp.zeros(acc_scr.shape, F32)
    thr_b = jnp.broadcast_to(thr, (tq, tk))

    def attn_body(j, carry):
        off = pl.multiple_of(j * tk, tk)
        bias = jnp.where(sc_scr[j] >= thr_b, 0.0, -jnp.inf)
        s = _dot_nt(qb_ref[...].reshape(DSA_HEADS * tq, LANES), kb_ref[pl.ds(off, tk), :])
        s = (s.reshape(DSA_HEADS, tq, tk) + bias[None]).reshape(DSA_HEADS * tq, tk)
        m_old = m_scr[...]
        m_new = jnp.maximum(m_old, jnp.max(s, axis=1, keepdims=True))
        p = jnp.exp2(s - jnp.concatenate([m_new] * (tk // LANES), axis=1))
        pv = jnp.dot(p.astype(BF16), va_ref[pl.ds(off, tk), :], preferred_element_type=F32)
        alpha = jnp.exp2(m_old - m_new)
        acc_scr[...] = jnp.concatenate([alpha, alpha], axis=1) * acc_scr[...] + pv
        m_scr[...] = m_new
        return carry

    lax.fori_loop(0, nvis, attn_body, 0)
    for h in range(DSA_HEADS):
        rs = slice(h * tq, (h + 1) * tq)
        o_ref[:, h * LANES:(h + 1) * LANES] = (acc_scr[rs, 0:LANES] / acc_scr[rs, LANES:2 * LANES]).astype(o_ref.dtype)


def _dsa(qi, qb, proj, kia, kib, kb, va):
    S = kb.shape[0]
    tq, tk = min(DSA_TQ, S), DSA_TK
    topk = min(DSA_TOPK, S // 4)
    nq = DSA_HEADS * DSA_HEAD_DIM
    full = lambda w: pl.BlockSpec((S, w), lambda i: (0, 0), pipeline_mode=pl.Buffered(1))
    return pl.pallas_call(
        functools.partial(_dsa_kernel, topk),
        out_shape=jax.ShapeDtypeStruct((S, nq), BF16),
        grid=(S // tq,),
        in_specs=[pl.BlockSpec((IDX_HEADS // 2, tq, LANES), lambda i: (0, i, 0)),
                  pl.BlockSpec((DSA_HEADS, tq, LANES), lambda i: (0, i, 0)),
                  pl.BlockSpec((tq, LANES), lambda i: (i, COL_TAIL // LANES)),
                  full(LANES), full(LANES), full(LANES), full(2 * LANES)],
        out_specs=pl.BlockSpec((tq, nq), lambda i: (i, 0)),
        scratch_shapes=[pltpu.VMEM((S // tk, tq, tk), F32),
                        pltpu.VMEM((IDX_HEADS, tq, LANES), F32),
                        pltpu.VMEM((tq, LANES), F32),
                        pltpu.VMEM((tq, LANES), F32),
                        pltpu.VMEM((DSA_HEADS * tq, LANES), F32),
                        pltpu.VMEM((DSA_HEADS * tq, 2 * LANES), F32)],
        compiler_params=_cparams(("arbitrary",)),
        name="dsa",
    )(qi, qb, proj, kia, kib, kb, va)


def _xattn_kernel(q_ref, kv_ref, o_ref):
    d = XATTN_HEAD_DIM
    nkv = XATTN_HEADS * d
    for h in range(XATTN_HEADS):
        q = q_ref[:, h * d:(h + 1) * d]
        k = kv_ref[:, h * d:(h + 1) * d]
        v = kv_ref[:, nkv + h * d:nkv + (h + 1) * d]
        s = _dot_nt(q, k) * (d ** -0.5)
        s = s - jnp.max(s, axis=1, keepdims=True)
        p = jnp.exp(s)
        p = p / jnp.sum(p, axis=1, keepdims=True)
        o_ref[:, h * d:(h + 1) * d] = jnp.dot(p.astype(BF16), v, preferred_element_type=F32).astype(o_ref.dtype)


def _xattn(q, kv, tm):
    S, N = q.shape
    M = kv.shape[0]
    tm = min(tm, S)
    return pl.pallas_call(
        _xattn_kernel,
        out_shape=jax.ShapeDtypeStruct((S, N), BF16),
        grid=(S // tm,),
        in_specs=[pl.BlockSpec((tm, N), lambda i: (i, 0)),
                  pl.BlockSpec((M, 2 * N), lambda i: (0, 0))],
        out_specs=pl.BlockSpec((tm, N), lambda i: (i, 0)),
        compiler_params=_cparams(("arbitrary",)),
        name="xattn",
    )(q, kv)


PEER_TS = 256


def _top16_distinct(s):
    vals, cnts = [], []
    rem = s
    for _ in range(PEER_TOPK):
        m = jnp.max(rem, axis=0, keepdims=True)
        eq = rem == m
        cnts.append(jnp.sum(jnp.where(eq, 1.0, 0.0), axis=0, keepdims=True))
        vals.append(m)
        rem = jnp.where(eq, -jnp.inf, rem)
    return vals, cnts


def _stack_rows(rows):
    n, T = len(rows), rows[0].shape[1]
    idx = lax.broadcasted_iota(jnp.int32, (n, T), 0)
    out = jnp.broadcast_to(rows[0], (n, T))
    for r in range(1, n):
        out = jnp.where(idx == r, rows[r], out)
    return out


def _peer_select_kernel(q_ref, sk_ref, c_ref, s2_ref, e1_ref, e2_ref):
    T = q_ref.shape[0]
    for h in range(PEER_HEADS):
        s1 = _dot_nt(sk_ref[h, 0], q_ref[:, (2 * h) * PEER_HALF:(2 * h + 1) * PEER_HALF])
        s2 = _dot_nt(sk_ref[h, 1], q_ref[:, (2 * h + 1) * PEER_HALF:(2 * h + 2) * PEER_HALF])
        a_vals, a_cnts = _top16_distinct(s1)
        b_vals, b_cnts = _top16_distinct(s2)
        b_mat = _stack_rows(b_vals)
        bc_mat = _stack_rows(b_cnts)
        cand = jnp.concatenate([a + b_mat for a in a_vals], axis=0)
        wgt = jnp.concatenate([c * bc_mat for c in a_cnts], axis=0)
        rem = cand
        n = jnp.zeros((1, T), F32)
        thr = jnp.full((1, T), -jnp.inf, F32)
        for _ in range(PEER_TOPK):
            m = jnp.max(rem, axis=0, keepdims=True)
            eq = rem == m
            thr = jnp.where(n < PEER_TOPK, m, thr)
            n = n + jnp.sum(jnp.where(eq, wgt, 0.0), axis=0, keepdims=True)
            rem = jnp.where(eq, -jnp.inf, rem)
        vmax = a_vals[0] + b_vals[0]
        keep = cand >= thr
        z = jnp.sum(jnp.where(keep, wgt * jnp.exp(cand - vmax), 0.0), axis=0, keepdims=True)
        c = jnp.full(s1.shape, jnp.inf, F32)
        for r in range(PEER_TOPK):
            rows = slice(r * PEER_TOPK, (r + 1) * PEER_TOPK)
            c_r = jnp.min(jnp.where(keep[rows], b_mat, jnp.inf), axis=0, keepdims=True)
            c = jnp.where(s1 == a_vals[r], c_r, c)
        c_ref[h] = c
        s2_ref[h] = s2
        e1_ref[h] = jnp.exp(s1 - a_vals[0])
        e2_ref[h] = jnp.exp(s2 - b_vals[0]) / z


def _peer_select(q, sub_keys):
    S = q.shape[0]
    ts = min(PEER_TS, S)
    big = jax.ShapeDtypeStruct((PEER_HEADS, PEER_NKEYS, S), F32)
    bspec = pl.BlockSpec((PEER_HEADS, PEER_NKEYS, ts), lambda i: (0, 0, i))
    return pl.pallas_call(
        _peer_select_kernel,
        out_shape=(big, big, big, big),
        grid=(S // ts,),
        in_specs=[pl.BlockSpec((ts, q.shape[1]), lambda i: (i, 0)),
                  pl.BlockSpec(sub_keys.shape, lambda i: (0, 0, 0, 0))],
        out_specs=(bspec, bspec, bspec, bspec),
        compiler_params=_cparams(("arbitrary",)),
        name="peer_select",
    )(q, sub_keys)


PEER_TM = 512
PEER_TE = 1024


def _gelu_tanh_scaled(x, g):
    inner = x * (0.7978845608028654 + 0.035677408136300125 * (x * x))
    return (g * (0.5 * x)) * (1.0 + jnp.tanh(inner))


def _peer_dense_kernel(xn_ref, u_ref, v_ref, c_ref, s2_ref, e1_ref, e2_ref, o_ref, wt_scr):
    e = pl.program_id(1)
    te, tm = u_ref.shape[0], xn_ref.shape[0]
    nk = PEER_NKEYS
    nr = te // nk
    c_rows = [[c_ref[h, pl.ds(e * nr + r, 1), :] for h in range(PEER_HEADS)] for r in range(nr)]
    e1_rows = [[e1_ref[h, pl.ds(e * nr + r, 1), :] for h in range(PEER_HEADS)] for r in range(nr)]
    tok_chunk = 2 * LANES
    for tc in range(tm // tok_chunk):
        ts = slice(tc * tok_chunk, (tc + 1) * tok_chunk)
        act = _dot_nt(u_ref[...], xn_ref[ts, :])
        for r in range(nr):
            for tb in range(tok_chunk // LANES):
                ls = slice(tc * tok_chunk + tb * LANES, tc * tok_chunk + (tb + 1) * LANES)
                g = jnp.zeros((nk, LANES), F32)
                for h in range(PEER_HEADS):
                    g = g + (jnp.where(s2_ref[h, :, ls] >= c_rows[r][h][:, ls], e2_ref[h, :, ls], 0.0)
                             * e1_rows[r][h][:, ls])
                w = _gelu_tanh_scaled(act[r * nk:(r + 1) * nk, tb * LANES:(tb + 1) * LANES], g)
                wt_scr[ls, r * nk:(r + 1) * nk] = w.T.astype(BF16)
    col_chunk = 2 * LANES
    for cc in range(o_ref.shape[1] // col_chunk):
        cs = slice(cc * col_chunk, (cc + 1) * col_chunk)
        contrib = jnp.dot(wt_scr[...], v_ref[:, cs], preferred_element_type=F32)
        prev = jnp.where(e == 0, 0.0, o_ref[:, cs])
        o_ref[:, cs] = prev + contrib


def _peer_dense(xn, u, v, c, s2, e1, e2):
    S, D = xn.shape
    E = u.shape[0]
    tm, te = min(PEER_TM, S), PEER_TE
    fac = pl.BlockSpec((PEER_HEADS, PEER_NKEYS, tm), lambda i, e: (0, 0, i))
    return pl.pallas_call(
        _peer_dense_kernel,
        out_shape=jax.ShapeDtypeStruct((S, D), F32),
        grid=(S // tm, E // te),
        in_specs=[pl.BlockSpec((tm, D), lambda i, e: (i, 0)),
                  pl.BlockSpec((te, D), lambda i, e: (e, 0)),
                  pl.BlockSpec((te, D), lambda i, e: (e, 0)),
                  fac, fac, fac, fac],
        out_specs=pl.BlockSpec((tm, D), lambda i, e: (i, 0)),
        scratch_shapes=[pltpu.VMEM((tm, te), BF16)],
        compiler_params=_cparams(("arbitrary", "arbitrary")),
        name="peer_dense",
    )(xn, u, v, c, s2, e1, e2)


def _final_kernel(h_ref, y_ref, g_ref, o_ref):
    x = h_ref[...] + y_ref[...]
    o_ref[...] = x * lax.rsqrt(jnp.mean(x * x, axis=-1, keepdims=True) + RMS_EPS) * g_ref[...]


def _final(h, y, g, tm):
    S, D = h.shape
    tm = min(tm, S)
    spec = pl.BlockSpec((tm, D), lambda i: (i, 0))
    return pl.pallas_call(
        _final_kernel,
        out_shape=jax.ShapeDtypeStruct((S, D), F32),
        grid=(S // tm,),
        in_specs=[spec, spec, pl.BlockSpec((1, D), lambda i: (0, 0))],
        out_specs=spec,
        compiler_params=_cparams(("arbitrary",)),
        name="final",
    )(h, y, g.reshape(1, D))


def _permute_w_in(w_in, l):
    c = _SRC
    sl = lambda a, b: w_in[l, :, c[a]:c[b]]
    pad = jnp.zeros((w_in.shape[1], NP_COLS - COL_TAIL - 96), w_in.dtype)
    cols = [sl("gv", "gr"), sl("gr", "glow"), sl("dq", "dk"), sl("ga", "gb"), sl("gb", "end"),
            sl("gq", "gk"), sl("gk", "gv"), sl("iq", "ik"), sl("dk", "dv"), sl("dv", "iq"),
            sl("ik", "iw"), sl("glow", "dq"), sl("iw", "ga"), pad]
    return jnp.concatenate(cols, axis=1).astype(BF16)


def _layer(h, mem, positions, norm_mix_g, w_in_perm, gla_gate_w2, gla_gate_b, gla_head_norm_g,
           w_proj_gla, w_proj_dsa, w_out, norm_x_g, norm_mem_g, w_xq, w_xk, w_xv, w_xo,
           norm_ffn_g, peer_wq, peer_sub_keys, peer_u, peer_v):
    (proj,) = _norm_mm(h, norm_mix_g, w_in_perm, F32, 1024, 1536, "in_proj")
    w2p = jnp.zeros((LANES, GLA_HEADS * GLA_DK), F32).at[TAIL_GLOW:TAIL_GLOW + GLA_GATE_RANK].set(gla_gate_w2)
    o_a = _gla(proj, w2p, gla_gate_b, gla_head_norm_g)
    qb, kb, vb, qi, kia, kib = _rope(proj, positions, 512)
    o_b = _dsa(qi, qb, proj, kia, kib, kb, vb)
    merged = _merge(o_a, o_b, w_proj_gla.astype(BF16), w_proj_dsa.astype(BF16), proj, 512, 2048)
    h = _mm_res(merged, w_out.astype(BF16), h, 512, 2048, "out_proj")

    (q_x,) = _norm_mm(h, norm_x_g, w_xq.astype(BF16), BF16, 512, 2048, "xattn_q")
    (kv_x,) = _norm_mm(mem, norm_mem_g, jnp.concatenate([w_xk, w_xv], axis=1).astype(BF16), BF16, 1024, 512,
                       "xattn_kv")
    o_x = _xattn(q_x, kv_x, 512)
    h = _mm_res(o_x, w_xo.astype(BF16), h, 512, 2048, "xattn_o")

    q_p, xn = _norm_mm(h, norm_ffn_g, peer_wq.astype(BF16), BF16, 512, 2048, "peer_q", keep_xn=True)
    c, s2, e1, e2 = _peer_select(q_p, peer_sub_keys.astype(BF16))
    y = _peer_dense(xn, peer_u.astype(BF16), peer_v.astype(BF16), c, s2, e1, e2)
    return h, y


def _take(a, i):
    return a.reshape(a.shape[1:]) if a.shape[0] == 1 else a[i]


def kernel(x, mem, positions, norm_mix_g, w_in, gla_gate_w2, gla_gate_b, gla_head_norm_g, w_proj_gla, w_proj_dsa, w_out, norm_x_g, norm_mem_g, w_xq, w_xk, w_xv, w_xo, norm_ffn_g, peer_wq, peer_sub_keys, peer_u, peer_v, norm_final_g):
    B, S, D = x.shape
    depth = w_in.shape[0]
    outs = []
    for b in range(B):
        h = _take(x, b)
        y = None
        for l in range(depth):
            if y is not None:
                h = h + y
            h, y = _layer(h, _take(mem, b), _take(positions, b), _take(norm_mix_g, l), _permute_w_in(w_in, l),
                          _take(gla_gate_w2, l), _take(gla_gate_b, l), _take(gla_head_norm_g, l),
                          _take(w_proj_gla, l), _take(w_proj_dsa, l), _take(w_out, l), _take(norm_x_g, l),
                          _take(norm_mem_g, l), _take(w_xq, l), _take(w_xk, l), _take(w_xv, l), _take(w_xo, l),
                          _take(norm_ffn_g, l), _take(peer_wq, l), _take(peer_sub_keys, l), _take(peer_u, l),
                          _take(peer_v, l))
        outs.append(_final(h, y, norm_final_g, 512))
    return jnp.stack(outs)
```

```python
import functools
import math

import numpy as np
import jax
import jax.numpy as jnp
from jax import lax
from jax.experimental import pallas as pl
from jax.experimental.pallas import tpu as pltpu

F32 = jnp.float32
BF16 = jnp.bfloat16

D_MODEL = 2048
GLA_HEADS = 4
GLA_DK = 256
GLA_DV = 512
GLA_GATE_RANK = 16
GLA_GATE_TEMP = 16.0
DSA_HEADS = 16
DSA_HEAD_DIM = 128
IDX_HEADS = 16
IDX_DIM = 64
DSA_TOPK = 256
XATTN_HEADS = 4
XATTN_HEAD_DIM = D_MODEL // XATTN_HEADS
PEER_HEADS = 8
PEER_NKEYS = 128
PEER_HALF = 128
PEER_TOPK = 16
ROPE_THETA = 10000.0
RMS_EPS = 1e-6

LANES = 128
V7X_VMEM_LIMIT_BYTES = 56 * 1024 * 1024

COL_GV = 0
COL_GR = 2048
COL_DQ = 4096
COL_GA = 6144
COL_GB = 8192
COL_GQ = 10240
COL_GK = 11264
COL_IQ = 12288
COL_DK = 13312
COL_DV = 13440
COL_TAIL = 13568
TAIL_GLOW = 64
TAIL_IW = 80
NP_COLS = 13824

_SRC = dict(gq=0, gk=1024, gv=2048, gr=4096, glow=6144, dq=6160, dk=8208, dv=8336,
            iq=8464, ik=9488, iw=9552, ga=9568, gb=11616, end=13664)


def _cparams(sem, vmem=V7X_VMEM_LIMIT_BYTES):
    return pltpu.CompilerParams(dimension_semantics=sem, vmem_limit_bytes=vmem)


def _weight_spec(shape, index_map, resident):
    if resident:
        return pl.BlockSpec(shape, index_map, pipeline_mode=pl.Buffered(1))
    return pl.BlockSpec(shape, index_map)


def _dot_nt(a, b):
    return lax.dot_general(a, b, (((1,), (1,)), ((), ())), preferred_element_type=F32)


def _split3(x):
    hi = x.astype(BF16)
    r1 = x - hi.astype(F32)
    mid = r1.astype(BF16)
    lo = (r1 - mid.astype(F32)).astype(BF16)
    return hi, mid, lo


def _norm_mm_kernel(keep_xn, x_ref, g_ref, w_ref, o_ref, *rest):
    a_scr = rest[-1]

    @pl.when(pl.program_id(1) == 0)
    def _():
        x = x_ref[...]
        y = x * lax.rsqrt(jnp.mean(x * x, axis=-1, keepdims=True) + RMS_EPS) * g_ref[...]
        a_scr[...] = y.astype(BF16)
        if keep_xn:
            rest[0][...] = a_scr[...]

    o_ref[...] = jnp.dot(a_scr[...], w_ref[...], preferred_element_type=F32).astype(o_ref.dtype)


def _norm_mm(x, g, w, out_dtype, tm, tn, name, keep_xn=False):
    M, K = x.shape
    N = w.shape[1]
    tm, tn = min(tm, M), min(tn, N)
    out_shape = [jax.ShapeDtypeStruct((M, N), out_dtype)]
    out_specs = [pl.BlockSpec((tm, tn), lambda i, j: (i, j))]
    if keep_xn:
        out_shape.append(jax.ShapeDtypeStruct((M, K), BF16))
        out_specs.append(pl.BlockSpec((tm, K), lambda i, j: (i, 0)))
    return pl.pallas_call(
        functools.partial(_norm_mm_kernel, keep_xn),
        out_shape=tuple(out_shape),
        grid=(M // tm, N // tn),
        in_specs=[pl.BlockSpec((tm, K), lambda i, j: (i, 0)),
                  pl.BlockSpec((1, K), lambda i, j: (0, 0)),
                  _weight_spec((K, tn), lambda i, j: (0, j), tn == N)],
        out_specs=tuple(out_specs),
        scratch_shapes=[pltpu.VMEM((tm, K), BF16)],
        compiler_params=_cparams(("arbitrary", "arbitrary")),
        name=name,
    )(x, g.reshape(1, K), w)


def _mm_res_kernel(a_ref, w_ref, r_ref, o_ref):
    o_ref[...] = r_ref[...] + jnp.dot(a_ref[...], w_ref[...], preferred_element_type=F32)


def _mm_res(a, w, res, tm, tn, name):
    M, K = a.shape
    N = w.shape[1]
    tm, tn = min(tm, M), min(tn, N)
    return pl.pallas_call(
        _mm_res_kernel,
        out_shape=jax.ShapeDtypeStruct((M, N), F32),
        grid=(M // tm, N // tn),
        in_specs=[pl.BlockSpec((tm, K), lambda i, j: (i, 0)),
                  _weight_spec((K, tn), lambda i, j: (0, j), tn == N),
                  pl.BlockSpec((tm, tn), lambda i, j: (i, j))],
        out_specs=pl.BlockSpec((tm, tn), lambda i, j: (i, j)),
        compiler_params=_cparams(("arbitrary", "arbitrary")),
        name=name,
    )(a, w, res)


def _merge_kernel(oa_ref, ob_ref, wa_ref, wb_ref, ga_ref, gb_ref, o_ref):
    ya = jnp.dot(oa_ref[...], wa_ref[...], preferred_element_type=F32)
    yb = jnp.dot(ob_ref[...], wb_ref[...], preferred_element_type=F32)
    o_ref[...] = (jax.nn.sigmoid(ga_ref[...]) * ya + jax.nn.sigmoid(gb_ref[...]) * yb).astype(o_ref.dtype)


def _merge(oa, ob, wa, wb, proj, tm, tn):
    M, K = oa.shape
    N = wa.shape[1]
    tm, tn = min(tm, M), min(tn, N)
    ca, cb = COL_GA // tn, COL_GB // tn
    return pl.pallas_call(
        _merge_kernel,
        out_shape=jax.ShapeDtypeStruct((M, N), BF16),
        grid=(M // tm, N // tn),
        in_specs=[pl.BlockSpec((tm, K), lambda i, j: (i, 0)),
                  pl.BlockSpec((tm, K), lambda i, j: (i, 0)),
                  _weight_spec((K, tn), lambda i, j: (0, j), tn == N),
                  _weight_spec((K, tn), lambda i, j: (0, j), tn == N),
                  pl.BlockSpec((tm, tn), lambda i, j: (i, ca + j)),
                  pl.BlockSpec((tm, tn), lambda i, j: (i, cb + j))],
        out_specs=pl.BlockSpec((tm, tn), lambda i, j: (i, j)),
        compiler_params=_cparams(("arbitrary", "arbitrary")),
        name="merge",
    )(oa, ob, wa, wb, proj, proj)


GLA_C = 256
GLA_LEVELS = (128, 64, 32, 16)
GLA_DIAG = 16


def _gla_matrices(C):
    t = np.arange(C)[:, None]
    u = np.arange(C)[None, :]
    L = (u <= t).astype(np.float32)
    mats = [L, (u > t).astype(np.float32)]
    for m in GLA_LEVELS:
        anchor = (t // (2 * m)) * (2 * m) + m - 1
        mats.append(L - (u <= anchor).astype(np.float32))
    anchor = (t // GLA_DIAG) * GLA_DIAG
    mats.append(L - (u <= anchor).astype(np.float32))
    return np.concatenate(mats, axis=0)


GLA_HPB = 4


def _gla_kernel(q_ref, k_ref, v_ref, r_ref, tail_ref, w2_ref, gb_ref, hg_ref, mats_ref, o_ref, st_scr):
    C = GLA_C

    @pl.when(pl.program_id(1) == 0)
    def _():
        st_scr[...] = jnp.zeros_like(st_scr)

    t_hi, t_mid, _ = _split3(tail_ref[...])
    mats = mats_ref[...]
    row = lax.broadcasted_iota(jnp.int32, (C, C), 0)
    col = lax.broadcasted_iota(jnp.int32, (C, C), 1)
    masks = []
    for m in GLA_LEVELS:
        sh = int(math.log2(2 * m))
        masks.append(((row >> sh) == (col >> sh)) & ((row & m) != 0) & ((col & m) == 0))
    sh = int(math.log2(GLA_DIAG))
    masks.append(((row >> sh) == (col >> sh)) & (col <= row))

    for hh in range(GLA_HPB):
        ks = slice(hh * GLA_DK, (hh + 1) * GLA_DK)
        vs = slice(hh * GLA_DV, (hh + 1) * GLA_DV)
        w_hi, w_mid, _ = _split3(w2_ref[:, ks])
        z = (jnp.dot(t_hi, w_hi, preferred_element_type=F32) + jnp.dot(t_hi, w_mid, preferred_element_type=F32)
             + jnp.dot(t_mid, w_hi, preferred_element_type=F32)) + gb_ref[:, ks]
        log_a = (jnp.minimum(z, 0.0) - jnp.log1p(jnp.exp(-jnp.abs(z)))) * (1.0 / GLA_GATE_TEMP)

        a_hi, a_mid, a_lo = _split3(log_a)
        dec = (jnp.dot(mats, a_hi, preferred_element_type=F32) + jnp.dot(mats, a_mid, preferred_element_type=F32)
               + jnp.dot(mats, a_lo, preferred_element_type=F32))
        b = dec[0:C]
        b_rest = dec[C:2 * C]

        q = q_ref[:, ks] * (GLA_DK ** -0.5)
        k = k_ref[:, ks]
        v = v_ref[:, vs]
        v_bf = v.astype(BF16)

        attn = jnp.zeros((C, C), F32)
        for li in range(len(GLA_LEVELS) + 1):
            d = dec[(2 + li) * C:(3 + li) * C]
            if li < len(GLA_LEVELS):
                qs = (q * jnp.exp(jnp.minimum(d, 0.0))).astype(BF16)
                ks_ = (k * jnp.exp(jnp.minimum(-d, 0.0))).astype(BF16)
            else:
                qs = (q * jnp.exp(d)).astype(BF16)
                ks_ = (k * jnp.exp(-d)).astype(BF16)
            attn = attn + jnp.where(masks[li], _dot_nt(qs, ks_), 0.0)

        st = st_scr[hh]
        o = _dot_nt((q * jnp.exp(b)).astype(BF16), st.astype(BF16))
        o = o + jnp.dot(attn.astype(BF16), v_bf, preferred_element_type=F32)

        kd = (k * jnp.exp(b_rest)).astype(BF16)
        upd = jnp.dot(v.T.astype(BF16), kd, preferred_element_type=F32)
        st_scr[hh] = st * jnp.exp(b[C - 1:C, :]) + upd

        y = o * lax.rsqrt(jnp.mean(o * o, axis=-1, keepdims=True) + RMS_EPS) * hg_ref[:, vs]
        r = r_ref[:, vs]
        o_ref[:, vs] = (y * (r * jax.nn.sigmoid(r))).astype(o_ref.dtype)


def _gla(proj, w2p, gate_b, head_g):
    S = proj.shape[0]
    C = GLA_C
    G = GLA_HPB
    mats = jnp.asarray(_gla_matrices(C), dtype=BF16)
    nm = mats.shape[0]
    wk, wv = G * GLA_DK, G * GLA_DV
    cq, ck = COL_GQ // wk, COL_GK // wk
    cv, cr = COL_GV // wv, COL_GR // wv
    ct = COL_TAIL // LANES
    return pl.pallas_call(
        _gla_kernel,
        out_shape=jax.ShapeDtypeStruct((S, GLA_HEADS * GLA_DV), BF16),
        grid=(GLA_HEADS // G, S // C),
        in_specs=[pl.BlockSpec((C, wk), lambda h, c: (c, cq + h)),
                  pl.BlockSpec((C, wk), lambda h, c: (c, ck + h)),
                  pl.BlockSpec((C, wv), lambda h, c: (c, cv + h)),
                  pl.BlockSpec((C, wv), lambda h, c: (c, cr + h)),
                  pl.BlockSpec((C, LANES), lambda h, c: (c, ct)),
                  pl.BlockSpec((LANES, wk), lambda h, c: (0, h)),
                  pl.BlockSpec((1, wk), lambda h, c: (0, h)),
                  pl.BlockSpec((1, wv), lambda h, c: (0, h)),
                  pl.BlockSpec((nm, C), lambda h, c: (0, 0))],
        out_specs=pl.BlockSpec((C, wv), lambda h, c: (c, h)),
        scratch_shapes=[pltpu.VMEM((G, GLA_DV, GLA_DK), F32)],
        compiler_params=_cparams(("arbitrary", "arbitrary")),
        name="gla",
    )(proj, proj, proj, proj, proj, w2p, gate_b.reshape(1, -1), head_g.reshape(1, -1), mats)


def _rope_kernel(pos_ref, dq_ref, dkv_ref, iq_ref, tail_ref, inv_ref, sgn_ref,
                 qb_ref, kb_ref, vb_ref, qi_ref, kia_ref, kib_ref):
    pos = pos_ref[...].astype(F32)
    lane = lax.broadcasted_iota(jnp.int32, (pos.shape[0], LANES), 1)

    ang = pos * inv_ref[0:1, :]
    c128 = jnp.cos(ang)
    s128 = jnp.sin(ang) * sgn_ref[0:1, :]
    ang = pos * inv_ref[1:2, :]
    c64 = jnp.cos(ang)
    s64 = jnp.sin(ang) * sgn_ref[1:2, :]

    def rot128(x):
        return x * c128 + pltpu.roll(x, 64, 1) * s128

    def rot64(x):
        swapped = jnp.where(lane % 64 < 32, pltpu.roll(x, 96, 1), pltpu.roll(x, 32, 1))
        return x * c64 + swapped * s64

    scale = (DSA_HEAD_DIM ** -0.5) * math.log2(math.e)
    for h in range(DSA_HEADS):
        qb_ref[h] = (rot128(dq_ref[:, h * LANES:(h + 1) * LANES]) * scale).astype(BF16)
    kb_ref[...] = rot128(dkv_ref[:, 0:LANES]).astype(BF16)
    vb_ref[:, 0:LANES] = dkv_ref[:, LANES:2 * LANES].astype(BF16)
    vb_ref[:, LANES:2 * LANES] = jnp.ones((pos.shape[0], LANES), BF16)
    for h in range(IDX_HEADS // 2):
        qi_ref[h] = rot64(iq_ref[:, h * LANES:(h + 1) * LANES]).astype(BF16)
    ki = jnp.where(lane < IDX_DIM, rot64(tail_ref[...]), 0.0)
    kia_ref[...] = ki.astype(BF16)
    kib_ref[...] = pltpu.roll(ki, 64, 1).astype(BF16)


def _rope(proj, positions, tm):
    S = proj.shape[0]
    tm = min(tm, S)
    inv128 = ROPE_THETA ** (-jnp.arange(0, DSA_HEAD_DIM, 2, dtype=F32) / DSA_HEAD_DIM)
    inv64 = ROPE_THETA ** (-jnp.arange(0, IDX_DIM, 2, dtype=F32) / IDX_DIM)
    inv = jnp.stack([jnp.tile(inv128, 2), jnp.tile(inv64, 4)])
    sgn = jnp.asarray(np.stack([np.repeat([-1.0, 1.0], 64), np.tile(np.repeat([-1.0, 1.0], 32), 2)]), F32)
    nq = DSA_HEADS * DSA_HEAD_DIM
    ni = IDX_HEADS * IDX_DIM
    outs = (jax.ShapeDtypeStruct((DSA_HEADS, S, LANES), BF16), jax.ShapeDtypeStruct((S, LANES), BF16),
            jax.ShapeDtypeStruct((S, 2 * LANES), BF16), jax.ShapeDtypeStruct((IDX_HEADS // 2, S, LANES), BF16),
            jax.ShapeDtypeStruct((S, LANES), BF16), jax.ShapeDtypeStruct((S, LANES), BF16))
    row = lambda w: pl.BlockSpec((tm, w), lambda i: (i, 0))
    heads = lambda n: pl.BlockSpec((n, tm, LANES), lambda i: (0, i, 0))
    return pl.pallas_call(
        _rope_kernel,
        out_shape=outs,
        grid=(S // tm,),
        in_specs=[pl.BlockSpec((tm, 1), lambda i: (i, 0)),
                  pl.BlockSpec((tm, nq), lambda i: (i, COL_DQ // nq)),
                  pl.BlockSpec((tm, 2 * LANES), lambda i: (i, COL_DK // (2 * LANES))),
                  pl.BlockSpec((tm, ni), lambda i: (i, COL_IQ // ni)),
                  pl.BlockSpec((tm, LANES), lambda i: (i, COL_TAIL // LANES)),
                  pl.BlockSpec((2, LANES), lambda i: (0, 0)),
                  pl.BlockSpec((2, LANES), lambda i: (0, 0))],
        out_specs=(heads(DSA_HEADS), row(LANES), row(2 * LANES), heads(IDX_HEADS // 2), row(LANES), row(LANES)),
        compiler_params=_cparams(("arbitrary",)),
        name="rope",
    )(positions.reshape(S, 1), proj, proj, proj, proj, inv, sgn)


DSA_TQ = 256
DSA_TK = 256
_SELECT_ALL = float(np.finfo(np.float32).min)


def _dsa_kernel(topk, qi_ref, qb_ref, tail_ref, kia_ref, kib_ref, kb_ref, va_ref, o_ref,
                sc_scr, wbc_scr, smin_scr, smax_scr, m_scr, acc_scr):
    tq, tk = qb_ref.shape[1], DSA_TK
    nhp = IDX_HEADS // 2
    i = pl.program_id(0)
    t0 = i * tq
    nvis = (t0 + tq - 1) // tk + 1

    w_idx = tail_ref[:, TAIL_IW:TAIL_IW + IDX_HEADS] * ((IDX_DIM ** -0.5) * (IDX_HEADS ** -0.5))
    for h in range(IDX_HEADS):
        wbc_scr[h] = jnp.broadcast_to(w_idx[:, h:h + 1], (tq, LANES))
    smin_scr[...] = jnp.full((tq, LANES), jnp.inf, F32)
    smax_scr[...] = jnp.full((tq, LANES), -jnp.inf, F32)
    t_row = t0 + lax.broadcasted_iota(jnp.int32, (tq, LANES), 0)
    lane = lax.broadcasted_iota(jnp.int32, (tq, LANES), 1)

    def score_body(j, carry):
        off = pl.multiple_of(j * tk, tk)
        qi2 = qi_ref[...].reshape(nhp * tq, LANES)
        s_a = _dot_nt(qi2, kia_ref[pl.ds(off, tk), :])
        s_b = _dot_nt(qi2, kib_ref[pl.ds(off, tk), :])
        for c in range(tk // LANES):
            cs = slice(c * LANES, (c + 1) * LANES)
            acc = jnp.zeros((tq, LANES), F32)
            for hp in range(nhp):
                rs = slice(hp * tq, (hp + 1) * tq)
                acc = acc + jnp.maximum(s_a[rs, cs], 0.0) * wbc_scr[2 * hp]
                acc = acc + jnp.maximum(s_b[rs, cs], 0.0) * wbc_scr[2 * hp + 1]
            vis = (off + c * LANES + lane) <= t_row
            sc_scr[j, :, cs] = jnp.where(vis, acc, -jnp.inf)
            smin_scr[...] = jnp.minimum(smin_scr[...], jnp.where(vis, acc, jnp.inf))
            smax_scr[...] = jnp.maximum(smax_scr[...], jnp.where(vis, acc, -jnp.inf))
        return carry

    lax.fori_loop(0, nvis, score_body, 0)
    smin = jnp.min(smin_scr[...], axis=1, keepdims=True)
    smax = jnp.max(smax_scr[...], axis=1, keepdims=True)

    n_vis = (t0 + lax.broadcasted_iota(jnp.int32, (tq, 1), 0) + 1).astype(F32)
    select_all = n_vis <= topk
    lo0 = jnp.where(select_all, _SELECT_ALL, smin)
    hi0 = jnp.where(select_all, _SELECT_ALL, smax)

    def count_ge(x):
        x_b = jnp.broadcast_to(x, (tq, LANES))

        def body(j, c):
            for cc in range(tk // LANES):
                c = c + jnp.where(sc_scr[j, :, cc * LANES:(cc + 1) * LANES] >= x_b, 1.0, 0.0)
            return c
        c = lax.fori_loop(0, nvis, body, jnp.zeros((tq, LANES), F32))
        return jnp.sum(c, axis=1, keepdims=True)

    def bis_cond(st):
        lo, hi, c_lo = st
        active = (lo < hi) & (c_lo != topk)
        return jnp.max(jnp.where(active, 1.0, 0.0)) > 0.5

    def bis_body(st):
        lo, hi, c_lo = st
        active = (lo < hi) & (c_lo != topk)
        mid = 0.5 * lo + 0.5 * hi
        stuck = (mid <= lo) | (mid >= hi)
        probe = jnp.where(active, jnp.where(stuck, hi, mid), lo)
        c = count_ge(probe)
        ge = c >= topk
        lo_n = jnp.where(active & ge, probe, lo)
        c_n = jnp.where(active & ge, c, c_lo)
        hi_n = jnp.where(active & jnp.logical_not(ge), jnp.where(stuck, lo, mid), hi)
        return lo_n, hi_n, c_n

    thr, _, _ = lax.while_loop(bis_cond, bis_body, (lo0, hi0, n_vis))

    m_scr[...] = jnp.full(m_scr.shape, -1e30, F32)
    acc_scr[...] = jnp.zeros(acc_scr.shape, F32)
    thr_b = jnp.broadcast_to(thr, (tq, tk))

    def attn_body(j, carry):
        off = pl.multiple_of(j * tk, tk)
        bias = jnp.where(sc_scr[j] >= thr_b, 0.0, -jnp.inf)
        s = _dot_nt(qb_ref[...].reshape(DSA_HEADS * tq, LANES), kb_ref[pl.ds(off, tk), :])
        s = (s.reshape(DSA_HEADS, tq, tk) + bias[None]).reshape(DSA_HEADS * tq, tk)
        m_old = m_scr[...]
        m_new = jnp.maximum(m_old, jnp.max(s, axis=1, keepdims=True))
        p = jnp.exp2(s - jnp.concatenate([m_new] * (tk // LANES), axis=1))
        pv = jnp.dot(p.astype(BF16), va_ref[pl.ds(off, tk), :], preferred_element_type=F32)
        alpha = jnp.exp2(m_old - m_new)
        acc_scr[...] = jnp.concatenate([alpha, alpha], axis=1) * acc_scr[...] + pv
        m_scr[...] = m_new
        return carry

    lax.fori_loop(0, nvis, attn_body, 0)
    for h in range(DSA_HEADS):
        rs = slice(h * tq, (h + 1) * tq)
        o_ref[:, h * LANES:(h + 1) * LANES] = (acc_scr[rs, 0:LANES] / acc_scr[rs, LANES:2 * LANES]).astype(o_ref.dtype)


def _dsa(qi, qb, proj, kia, kib, kb, va):
    S = kb.shape[0]
    tq, tk = min(DSA_TQ, S), DSA_TK
    topk = min(DSA_TOPK, S // 4)
    nq = DSA_HEADS * DSA_HEAD_DIM
    full = lambda w: pl.BlockSpec((S, w), lambda i: (0, 0), pipeline_mode=pl.Buffered(1))
    return pl.pallas_call(
        functools.partial(_dsa_kernel, topk),
        out_shape=jax.ShapeDtypeStruct((S, nq), BF16),
        grid=(S // tq,),
        in_specs=[pl.BlockSpec((IDX_HEADS // 2, tq, LANES), lambda i: (0, i, 0)),
                  pl.BlockSpec((DSA_HEADS, tq, LANES), lambda i: (0, i, 0)),
                  pl.BlockSpec((tq, LANES), lambda i: (i, COL_TAIL // LANES)),
                  full(LANES), full(LANES), full(LANES), full(2 * LANES)],
        out_specs=pl.BlockSpec((tq, nq), lambda i: (i, 0)),
        scratch_shapes=[pltpu.VMEM((S // tk, tq, tk), F32),
                        pltpu.VMEM((IDX_HEADS, tq, LANES), F32),
                        pltpu.VMEM((tq, LANES), F32),
                        pltpu.VMEM((tq, LANES), F32),
                        pltpu.VMEM((DSA_HEADS * tq, LANES), F32),
                        pltpu.VMEM((DSA_HEADS * tq, 2 * LANES), F32)],
        compiler_params=_cparams(("arbitrary",)),
        name="dsa",
    )(qi, qb, proj, kia, kib, kb, va)


def _xattn_kernel(q_ref, kv_ref, o_ref):
    d = XATTN_HEAD_DIM
    nkv = XATTN_HEADS * d
    for h in range(XATTN_HEADS):
        q = q_ref[:, h * d:(h + 1) * d]
        k = kv_ref[:, h * d:(h + 1) * d]
        v = kv_ref[:, nkv + h * d:nkv + (h + 1) * d]
        s = _dot_nt(q, k) * (d ** -0.5)
        s = s - jnp.max(s, axis=1, keepdims=True)
        p = jnp.exp(s)
        p = p / jnp.sum(p, axis=1, keepdims=True)
        o_ref[:, h * d:(h + 1) * d] = jnp.dot(p.astype(BF16), v, preferred_element_type=F32).astype(o_ref.dtype)


def _xattn(q, kv, tm):
    S, N = q.shape
    M = kv.shape[0]
    tm = min(tm, S)
    return pl.pallas_call(
        _xattn_kernel,
        out_shape=jax.ShapeDtypeStruct((S, N), BF16),
        grid=(S // tm,),
        in_specs=[pl.BlockSpec((tm, N), lambda i: (i, 0)),
                  pl.BlockSpec((M, 2 * N), lambda i: (0, 0))],
        out_specs=pl.BlockSpec((tm, N), lambda i: (i, 0)),
        compiler_params=_cparams(("arbitrary",)),
        name="xattn",
    )(q, kv)


PEER_TS = 256


def _top16_distinct(s):
    vals, cnts = [], []
    rem = s
    for _ in range(PEER_TOPK):
        m = jnp.max(rem, axis=0, keepdims=True)
        eq = rem == m
        cnts.append(jnp.sum(jnp.where(eq, 1.0, 0.0), axis=0, keepdims=True))
        vals.append(m)
        rem = jnp.where(eq, -jnp.inf, rem)
    return vals, cnts


def _stack_rows(rows):
    n, T = len(rows), rows[0].shape[1]
    idx = lax.broadcasted_iota(jnp.int32, (n, T), 0)
    out = jnp.broadcast_to(rows[0], (n, T))
    for r in range(1, n):
        out = jnp.where(idx == r, rows[r], out)
    return out


def _peer_select_kernel(q_ref, sk_ref, c_ref, s2_ref, e1_ref, e2_ref):
    T = q_ref.shape[0]
    for h in range(PEER_HEADS):
        s1 = _dot_nt(sk_ref[h, 0], q_ref[:, (2 * h) * PEER_HALF:(2 * h + 1) * PEER_HALF])
        s2 = _dot_nt(sk_ref[h, 1], q_ref[:, (2 * h + 1) * PEER_HALF:(2 * h + 2) * PEER_HALF])
        a_vals, a_cnts = _top16_distinct(s1)
        b_vals, b_cnts = _top16_distinct(s2)
        a_mat, ac_mat = _stack_rows(a_vals), _stack_rows(a_cnts)
        b_mat, bc_mat = _stack_rows(b_vals), _stack_rows(b_cnts)
        k = PEER_TOPK
        row8 = lax.broadcasted_iota(jnp.int32, (8, T), 0)
        row16 = lax.broadcasted_iota(jnp.int32, (k, T), 0)
        ninf = -jnp.inf
        pieces = [
            (a_vals[0] + b_mat, a_cnts[0] * bc_mat),
            (a_vals[1] + b_mat[0:8], a_cnts[1] * bc_mat[0:8]),
            (jnp.where(row8 < k // 3, a_vals[2] + b_mat[0:8], ninf), a_cnts[2] * bc_mat[0:8]),
            (jnp.where(row8 < k // 4, a_vals[3] + b_mat[0:8], ninf), a_cnts[3] * bc_mat[0:8]),
            (jnp.where(row16 >= 4, b_vals[0] + a_mat, ninf), b_cnts[0] * ac_mat),
            (jnp.where(row8 >= 4, b_vals[1] + a_mat[0:8], ninf), b_cnts[1] * ac_mat[0:8]),
            (jnp.where(row8 == 4, b_vals[2] + a_mat[0:8], ninf), b_cnts[2] * ac_mat[0:8]),
        ]
        cand = jnp.concatenate([p[0] for p in pieces], axis=0)
        wgt = jnp.concatenate([p[1] for p in pieces], axis=0)
        rem = cand
        n = jnp.zeros((1, T), F32)
        thr = jnp.full((1, T), -jnp.inf, F32)
        for _ in range(PEER_TOPK):
            m = jnp.max(rem, axis=0, keepdims=True)
            eq = rem == m
            thr = jnp.where(n < PEER_TOPK, m, thr)
            n = n + jnp.sum(jnp.where(eq, wgt, 0.0), axis=0, keepdims=True)
            rem = jnp.where(eq, -jnp.inf, rem)
        vmax = a_vals[0] + b_vals[0]
        keep = cand >= thr
        z = jnp.sum(jnp.where(keep, wgt * jnp.exp(cand - vmax), 0.0), axis=0, keepdims=True)
        inf = jnp.inf
        c_rows = [jnp.min(jnp.where(keep[0:16], b_mat, inf), axis=0, keepdims=True),
                  jnp.min(jnp.where(keep[16:24], b_mat[0:8], inf), axis=0, keepdims=True),
                  jnp.min(jnp.where(keep[24:32], b_mat[0:8], inf), axis=0, keepdims=True),
                  jnp.min(jnp.where(keep[32:40], b_mat[0:8], inf), axis=0, keepdims=True)]
        c_hi = jnp.where(keep[40:56], b_vals[0], inf)
        c_hi = jnp.minimum(c_hi, jnp.concatenate(
            [jnp.minimum(jnp.where(keep[56:64], b_vals[1], inf), jnp.where(keep[64:72], b_vals[2], inf)),
             jnp.full((8, T), inf, F32)], axis=0))
        c = jnp.full(s1.shape, inf, F32)
        for r in range(PEER_TOPK):
            c_r = c_rows[r] if r < 4 else c_hi[r:r + 1]
            c = jnp.where(s1 == a_vals[r], c_r, c)
        c_ref[h] = c
        s2_ref[h] = s2
        e1_ref[h] = jnp.exp(s1 - a_vals[0])
        e2_ref[h] = jnp.exp(s2 - b_vals[0]) / z


def _peer_select(q, sub_keys):
    S = q.shape[0]
    ts = min(PEER_TS, S)
    big = jax.ShapeDtypeStruct((PEER_HEADS, PEER_NKEYS, S), F32)
    bspec = pl.BlockSpec((PEER_HEADS, PEER_NKEYS, ts), lambda i: (0, 0, i))
    return pl.pallas_call(
        _peer_select_kernel,
        out_shape=(big, big, big, big),
        grid=(S // ts,),
        in_specs=[pl.BlockSpec((ts, q.shape[1]), lambda i: (i, 0)),
                  pl.BlockSpec(sub_keys.shape, lambda i: (0, 0, 0, 0))],
        out_specs=(bspec, bspec, bspec, bspec),
        compiler_params=_cparams(("arbitrary",)),
        name="peer_select",
    )(q, sub_keys)


PEER_TM = 512
PEER_TE = 1024


def _gelu_tanh_scaled(x, g):
    inner = x * (0.7978845608028654 + 0.035677408136300125 * (x * x))
    return (g * (0.5 * x)) * (1.0 + jnp.tanh(inner))


def _peer_dense_kernel(xn_ref, u_ref, v_ref, c_ref, s2_ref, e1_ref, e2_ref, o_ref, wt_scr):
    e = pl.program_id(1)
    te, tm = u_ref.shape[0], xn_ref.shape[0]
    nk = PEER_NKEYS
    nr = te // nk
    c_rows = [[c_ref[h, pl.ds(e * nr + r, 1), :] for h in range(PEER_HEADS)] for r in range(nr)]
    e1_rows = [[e1_ref[h, pl.ds(e * nr + r, 1), :] for h in range(PEER_HEADS)] for r in range(nr)]
    tok_chunk = 2 * LANES
    for tc in range(tm // tok_chunk):
        ts = slice(tc * tok_chunk, (tc + 1) * tok_chunk)
        act = _dot_nt(u_ref[...], xn_ref[ts, :])
        for r in range(nr):
            for tb in range(tok_chunk // LANES):
                ls = slice(tc * tok_chunk + tb * LANES, tc * tok_chunk + (tb + 1) * LANES)
                g = jnp.zeros((nk, LANES), F32)
                for h in range(PEER_HEADS):
                    g = g + (jnp.where(s2_ref[h, :, ls] >= c_rows[r][h][:, ls], e2_ref[h, :, ls], 0.0)
                             * e1_rows[r][h][:, ls])
                w = _gelu_tanh_scaled(act[r * nk:(r + 1) * nk, tb * LANES:(tb + 1) * LANES], g)
                wt_scr[ls, r * nk:(r + 1) * nk] = w.T.astype(BF16)
    col_chunk = 2 * LANES
    for cc in range(o_ref.shape[1] // col_chunk):
        cs = slice(cc * col_chunk, (cc + 1) * col_chunk)
        contrib = jnp.dot(wt_scr[...], v_ref[:, cs], preferred_element_type=F32)
        prev = jnp.where(e == 0, 0.0, o_ref[:, cs])
        o_ref[:, cs] = prev + contrib


def _peer_dense(xn, u, v, c, s2, e1, e2):
    S, D = xn.shape
    E = u.shape[0]
    tm, te = min(PEER_TM, S), PEER_TE
    fac = pl.BlockSpec((PEER_HEADS, PEER_NKEYS, tm), lambda i, e: (0, 0, i))
    return pl.pallas_call(
        _peer_dense_kernel,
        out_shape=jax.ShapeDtypeStruct((S, D), F32),
        grid=(S // tm, E // te),
        in_specs=[pl.BlockSpec((tm, D), lambda i, e: (i, 0)),
                  pl.BlockSpec((te, D), lambda i, e: (e, 0)),
                  pl.BlockSpec((te, D), lambda i, e: (e, 0)),
                  fac, fac, fac, fac],
        out_specs=pl.BlockSpec((tm, D), lambda i, e: (i, 0)),
        scratch_shapes=[pltpu.VMEM((tm, te), BF16)],
        compiler_params=_cparams(("arbitrary", "arbitrary")),
        name="peer_dense",
    )(xn, u, v, c, s2, e1, e2)


def _final_kernel(h_ref, y_ref, g_ref, o_ref):
    x = h_ref[...] + y_ref[...]
    o_ref[...] = x * lax.rsqrt(jnp.mean(x * x, axis=-1, keepdims=True) + RMS_EPS) * g_ref[...]


def _final(h, y, g, tm):
    S, D = h.shape
    tm = min(tm, S)
    spec = pl.BlockSpec((tm, D), lambda i: (i, 0))
    return pl.pallas_call(
        _final_kernel,
        out_shape=jax.ShapeDtypeStruct((S, D), F32),
        grid=(S // tm,),
        in_specs=[spec, spec, pl.BlockSpec((1, D), lambda i: (0, 0))],
        out_specs=spec,
        compiler_params=_cparams(("arbitrary",)),
        name="final",
    )(h, y, g.reshape(1, D))


def _permute_w_in(w_in, l):
    c = _SRC
    sl = lambda a, b: w_in[l, :, c[a]:c[b]]
    pad = jnp.zeros((w_in.shape[1], NP_COLS - COL_TAIL - 96), w_in.dtype)
    cols = [sl("gv", "gr"), sl("gr", "glow"), sl("dq", "dk"), sl("ga", "gb"), sl("gb", "end"),
            sl("gq", "gk"), sl("gk", "gv"), sl("iq", "ik"), sl("dk", "dv"), sl("dv", "iq"),
            sl("ik", "iw"), sl("glow", "dq"), sl("iw", "ga"), pad]
    return jnp.concatenate(cols, axis=1).astype(BF16)


def _layer(h, mem, positions, norm_mix_g, w_in_perm, gla_gate_w2, gla_gate_b, gla_head_norm_g,
           w_proj_gla, w_proj_dsa, w_out, norm_x_g, norm_mem_g, w_xq, w_xk, w_xv, w_xo,
           norm_ffn_g, peer_wq, peer_sub_keys, peer_u, peer_v):
    (proj,) = _norm_mm(h, norm_mix_g, w_in_perm, F32, 1024, 1536, "in_proj")
    w2p = jnp.zeros((LANES, GLA_HEADS * GLA_DK), F32).at[TAIL_GLOW:TAIL_GLOW + GLA_GATE_RANK].set(gla_gate_w2)
    o_a = _gla(proj, w2p, gla_gate_b, gla_head_norm_g)
    qb, kb, vb, qi, kia, kib = _rope(proj, positions, 512)
    o_b = _dsa(qi, qb, proj, kia, kib, kb, vb)
    merged = _merge(o_a, o_b, w_proj_gla.astype(BF16), w_proj_dsa.astype(BF16), proj, 512, 2048)
    h = _mm_res(merged, w_out.astype(BF16), h, 512, 2048, "out_proj")

    (q_x,) = _norm_mm(h, norm_x_g, w_xq.astype(BF16), BF16, 512, 2048, "xattn_q")
    (kv_x,) = _norm_mm(mem, norm_mem_g, jnp.concatenate([w_xk, w_xv], axis=1).astype(BF16), BF16, 1024, 512,
                       "xattn_kv")
    o_x = _xattn(q_x, kv_x, 512)
    h = _mm_res(o_x, w_xo.astype(BF16), h, 512, 2048, "xattn_o")

    q_p, xn = _norm_mm(h, norm_ffn_g, peer_wq.astype(BF16), BF16, 512, 2048, "peer_q", keep_xn=True)
    c, s2, e1, e2 = _peer_select(q_p, peer_sub_keys.astype(BF16))
    y = _peer_dense(xn, peer_u.astype(BF16), peer_v.astype(BF16), c, s2, e1, e2)
    return h, y


def _take(a, i):
    return a.reshape(a.shape[1:]) if a.shape[0] == 1 else a[i]


def kernel(x, mem, positions, norm_mix_g, w_in, gla_gate_w2, gla_gate_b, gla_head_norm_g, w_proj_gla, w_proj_dsa, w_out, norm_x_g, norm_mem_g, w_xq, w_xk, w_xv, w_xo, norm_ffn_g, peer_wq, peer_sub_keys, peer_u, peer_v, norm_final_g):
    B, S, D = x.shape
    depth = w_in.shape[0]
    outs = []
    for b in range(B):
        h = _take(x, b)
        y = None
        for l in range(depth):
            if y is not None:
                h = h + y
            h, y = _layer(h, _take(mem, b), _take(positions, b), _take(norm_mix_g, l), _permute_w_in(w_in, l),
                          _take(gla_gate_w2, l), _take(gla_gate_b, l), _take(gla_head_norm_g, l),
                          _take(w_proj_gla, l), _take(w_proj_dsa, l), _take(w_out, l), _take(norm_x_g, l),
                          _take(norm_mem_g, l), _take(w_xq, l), _take(w_xk, l), _take(w_xv, l), _take(w_xo, l),
                          _take(norm_ffn_g, l), _take(peer_wq, l), _take(peer_sub_keys, l), _take(peer_u, l),
                          _take(peer_v, l))
        outs.append(_final(h, y, norm_final_g, 512))
    return jnp.stack(outs)
```

```python
import functools
import math

import numpy as np
import jax
import jax.numpy as jnp
from jax import lax
from jax.experimental import pallas as pl
from jax.experimental.pallas import tpu as pltpu

F32 = jnp.float32
BF16 = jnp.bfloat16

D_MODEL = 2048
GLA_HEADS = 4
GLA_DK = 256
GLA_DV = 512
GLA_GATE_RANK = 16
GLA_GATE_TEMP = 16.0
DSA_HEADS = 16
DSA_HEAD_DIM = 128
IDX_HEADS = 16
IDX_DIM = 64
DSA_TOPK = 256
XATTN_HEADS = 4
XATTN_HEAD_DIM = D_MODEL // XATTN_HEADS
PEER_HEADS = 8
PEER_NKEYS = 128
PEER_HALF = 128
PEER_TOPK = 16
ROPE_THETA = 10000.0
RMS_EPS = 1e-6

LANES = 128
V7X_VMEM_LIMIT_BYTES = 56 * 1024 * 1024

COL_GV = 0
COL_GR = 2048
COL_DQ = 4096
COL_GA = 6144
COL_GB = 8192
COL_GQ = 10240
COL_GK = 11264
COL_IQ = 12288
COL_DK = 13312
COL_DV = 13440
COL_TAIL = 13568
TAIL_GLOW = 64
TAIL_IW = 80
NP_COLS = 13824

_SRC = dict(gq=0, gk=1024, gv=2048, gr=4096, glow=6144, dq=6160, dk=8208, dv=8336,
            iq=8464, ik=9488, iw=9552, ga=9568, gb=11616, end=13664)


def _cparams(sem, vmem=V7X_VMEM_LIMIT_BYTES):
    return pltpu.CompilerParams(dimension_semantics=sem, vmem_limit_bytes=vmem)


def _weight_spec(shape, index_map, resident):
    if resident:
        return pl.BlockSpec(shape, index_map, pipeline_mode=pl.Buffered(1))
    return pl.BlockSpec(shape, index_map)


def _dot_nt(a, b):
    return lax.dot_general(a, b, (((1,), (1,)), ((), ())), preferred_element_type=F32)


def _split3(x):
    hi = x.astype(BF16)
    r1 = x - hi.astype(F32)
    mid = r1.astype(BF16)
    lo = (r1 - mid.astype(F32)).astype(BF16)
    return hi, mid, lo


def _norm_mm_kernel(keep_xn, x_ref, g_ref, w_ref, o_ref, *rest):
    a_scr = rest[-1]

    @pl.when(pl.program_id(1) == 0)
    def _():
        x = x_ref[...]
        y = x * lax.rsqrt(jnp.mean(x * x, axis=-1, keepdims=True) + RMS_EPS) * g_ref[...]
        a_scr[...] = y.astype(BF16)
        if keep_xn:
            rest[0][...] = a_scr[...]

    o_ref[...] = jnp.dot(a_scr[...], w_ref[...], preferred_element_type=F32).astype(o_ref.dtype)


def _norm_mm(x, g, w, out_dtype, tm, tn, name, keep_xn=False):
    M, K = x.shape
    N = w.shape[1]
    tm, tn = min(tm, M), min(tn, N)
    out_shape = [jax.ShapeDtypeStruct((M, N), out_dtype)]
    out_specs = [pl.BlockSpec((tm, tn), lambda i, j: (i, j))]
    if keep_xn:
        out_shape.append(jax.ShapeDtypeStruct((M, K), BF16))
        out_specs.append(pl.BlockSpec((tm, K), lambda i, j: (i, 0)))
    return pl.pallas_call(
        functools.partial(_norm_mm_kernel, keep_xn),
        out_shape=tuple(out_shape),
        grid=(M // tm, N // tn),
        in_specs=[pl.BlockSpec((tm, K), lambda i, j: (i, 0)),
                  pl.BlockSpec((1, K), lambda i, j: (0, 0)),
                  _weight_spec((K, tn), lambda i, j: (0, j), tn == N)],
        out_specs=tuple(out_specs),
        scratch_shapes=[pltpu.VMEM((tm, K), BF16)],
        compiler_params=_cparams(("arbitrary", "arbitrary")),
        name=name,
    )(x, g.reshape(1, K), w)


def _mm_res_kernel(a_ref, w_ref, r_ref, o_ref):
    o_ref[...] = r_ref[...] + jnp.dot(a_ref[...], w_ref[...], preferred_element_type=F32)


def _mm_res(a, w, res, tm, tn, name):
    M, K = a.shape
    N = w.shape[1]
    tm, tn = min(tm, M), min(tn, N)
    return pl.pallas_call(
        _mm_res_kernel,
        out_shape=jax.ShapeDtypeStruct((M, N), F32),
        grid=(M // tm, N // tn),
        in_specs=[pl.BlockSpec((tm, K), lambda i, j: (i, 0)),
                  _weight_spec((K, tn), lambda i, j: (0, j), tn == N),
                  pl.BlockSpec((tm, tn), lambda i, j: (i, j))],
        out_specs=pl.BlockSpec((tm, tn), lambda i, j: (i, j)),
        compiler_params=_cparams(("arbitrary", "arbitrary")),
        name=name,
    )(a, w, res)


def _merge_kernel(oa_ref, ob_ref, wa_ref, wb_ref, ga_ref, gb_ref, o_ref):
    ya = jnp.dot(oa_ref[...], wa_ref[...], preferred_element_type=F32)
    yb = jnp.dot(ob_ref[...], wb_ref[...], preferred_element_type=F32)
    o_ref[...] = (jax.nn.sigmoid(ga_ref[...]) * ya + jax.nn.sigmoid(gb_ref[...]) * yb).astype(o_ref.dtype)


def _merge(oa, ob, wa, wb, proj, tm, tn):
    M, K = oa.shape
    N = wa.shape[1]
    tm, tn = min(tm, M), min(tn, N)
    ca, cb = COL_GA // tn, COL_GB // tn
    return pl.pallas_call(
        _merge_kernel,
        out_shape=jax.ShapeDtypeStruct((M, N), BF16),
        grid=(M // tm, N // tn),
        in_specs=[pl.BlockSpec((tm, K), lambda i, j: (i, 0)),
                  pl.BlockSpec((tm, K), lambda i, j: (i, 0)),
                  _weight_spec((K, tn), lambda i, j: (0, j), tn == N),
                  _weight_spec((K, tn), lambda i, j: (0, j), tn == N),
                  pl.BlockSpec((tm, tn), lambda i, j: (i, ca + j)),
                  pl.BlockSpec((tm, tn), lambda i, j: (i, cb + j))],
        out_specs=pl.BlockSpec((tm, tn), lambda i, j: (i, j)),
        compiler_params=_cparams(("arbitrary", "arbitrary")),
        name="merge",
    )(oa, ob, wa, wb, proj, proj)


GLA_C = 256
GLA_LEVELS = (128, 64, 32, 16)
GLA_DIAG = 16


def _gla_matrices(C):
    t = np.arange(C)[:, None]
    u = np.arange(C)[None, :]
    L = (u <= t).astype(np.float32)
    mats = [L, (u > t).astype(np.float32)]
    for m in GLA_LEVELS:
        anchor = (t // (2 * m)) * (2 * m) + m - 1
        mats.append(L - (u <= anchor).astype(np.float32))
    anchor = (t // GLA_DIAG) * GLA_DIAG
    mats.append(L - (u <= anchor).astype(np.float32))
    return np.concatenate(mats, axis=0)


GLA_HPB = 4


def _gla_kernel(q_ref, k_ref, v_ref, r_ref, tail_ref, w2_ref, gb_ref, hg_ref, mats_ref, o_ref, st_scr):
    C = GLA_C

    @pl.when(pl.program_id(1) == 0)
    def _():
        st_scr[...] = jnp.zeros_like(st_scr)

    t_hi, t_mid, _ = _split3(tail_ref[...])
    mats = mats_ref[...]
    row = lax.broadcasted_iota(jnp.int32, (C, C), 0)
    col = lax.broadcasted_iota(jnp.int32, (C, C), 1)
    masks = []
    for m in GLA_LEVELS:
        sh = int(math.log2(2 * m))
        masks.append(((row >> sh) == (col >> sh)) & ((row & m) != 0) & ((col & m) == 0))
    sh = int(math.log2(GLA_DIAG))
    masks.append(((row >> sh) == (col >> sh)) & (col <= row))

    for hh in range(GLA_HPB):
        ks = slice(hh * GLA_DK, (hh + 1) * GLA_DK)
        vs = slice(hh * GLA_DV, (hh + 1) * GLA_DV)
        w_hi, w_mid, _ = _split3(w2_ref[:, ks])
        z = (jnp.dot(t_hi, w_hi, preferred_element_type=F32) + jnp.dot(t_hi, w_mid, preferred_element_type=F32)
             + jnp.dot(t_mid, w_hi, preferred_element_type=F32)) + gb_ref[:, ks]
        log_a = (jnp.minimum(z, 0.0) - jnp.log1p(jnp.exp(-jnp.abs(z)))) * (1.0 / GLA_GATE_TEMP)

        a_hi, a_mid, a_lo = _split3(log_a)
        dec = (jnp.dot(mats, a_hi, preferred_element_type=F32) + jnp.dot(mats, a_mid, preferred_element_type=F32)
               + jnp.dot(mats, a_lo, preferred_element_type=F32))
        b = dec[0:C]
        b_rest = dec[C:2 * C]

        q = q_ref[:, ks] * (GLA_DK ** -0.5)
        k = k_ref[:, ks]
        v = v_ref[:, vs]
        v_bf = v.astype(BF16)

        attn = jnp.zeros((C, C), F32)
        for li in range(len(GLA_LEVELS) + 1):
            d = dec[(2 + li) * C:(3 + li) * C]
            if li < len(GLA_LEVELS):
                qs = (q * jnp.exp(jnp.minimum(d, 0.0))).astype(BF16)
                ks_ = (k * jnp.exp(jnp.minimum(-d, 0.0))).astype(BF16)
            else:
                qs = (q * jnp.exp(d)).astype(BF16)
                ks_ = (k * jnp.exp(-d)).astype(BF16)
            attn = attn + jnp.where(masks[li], _dot_nt(qs, ks_), 0.0)

        st = st_scr[hh]
        o = _dot_nt((q * jnp.exp(b)).astype(BF16), st.astype(BF16))
        o = o + jnp.dot(attn.astype(BF16), v_bf, preferred_element_type=F32)

        kd = (k * jnp.exp(b_rest)).astype(BF16)
        upd = jnp.dot(v.T.astype(BF16), kd, preferred_element_type=F32)
        st_scr[hh] = st * jnp.exp(b[C - 1:C, :]) + upd

        y = o * lax.rsqrt(jnp.mean(o * o, axis=-1, keepdims=True) + RMS_EPS) * hg_ref[:, vs]
        r = r_ref[:, vs]
        o_ref[:, vs] = (y * (r * jax.nn.sigmoid(r))).astype(o_ref.dtype)


def _gla(proj, w2p, gate_b, head_g):
    S = proj.shape[0]
    C = GLA_C
    G = GLA_HPB
    mats = jnp.asarray(_gla_matrices(C), dtype=BF16)
    nm = mats.shape[0]
    wk, wv = G * GLA_DK, G * GLA_DV
    cq, ck = COL_GQ // wk, COL_GK // wk
    cv, cr = COL_GV // wv, COL_GR // wv
    ct = COL_TAIL // LANES
    return pl.pallas_call(
        _gla_kernel,
        out_shape=jax.ShapeDtypeStruct((S, GLA_HEADS * GLA_DV), BF16),
        grid=(GLA_HEADS // G, S // C),
        in_specs=[pl.BlockSpec((C, wk), lambda h, c: (c, cq + h)),
                  pl.BlockSpec((C, wk), lambda h, c: (c, ck + h)),
                  pl.BlockSpec((C, wv), lambda h, c: (c, cv + h)),
                  pl.BlockSpec((C, wv), lambda h, c: (c, cr + h)),
                  pl.BlockSpec((C, LANES), lambda h, c: (c, ct)),
                  pl.BlockSpec((LANES, wk), lambda h, c: (0, h)),
                  pl.BlockSpec((1, wk), lambda h, c: (0, h)),
                  pl.BlockSpec((1, wv), lambda h, c: (0, h)),
                  pl.BlockSpec((nm, C), lambda h, c: (0, 0))],
        out_specs=pl.BlockSpec((C, wv), lambda h, c: (c, h)),
        scratch_shapes=[pltpu.VMEM((G, GLA_DV, GLA_DK), F32)],
        compiler_params=_cparams(("arbitrary", "arbitrary")),
        name="gla",
    )(proj, proj, proj, proj, proj, w2p, gate_b.reshape(1, -1), head_g.reshape(1, -1), mats)


def _rope_kernel(pos_ref, dq_ref, dkv_ref, iq_ref, tail_ref, inv_ref, sgn_ref,
                 qb_ref, kb_ref, vb_ref, qi_ref, kia_ref, kib_ref):
    pos = pos_ref[...].astype(F32)
    lane = lax.broadcasted_iota(jnp.int32, (pos.shape[0], LANES), 1)

    ang = pos * inv_ref[0:1, :]
    c128 = jnp.cos(ang)
    s128 = jnp.sin(ang) * sgn_ref[0:1, :]
    ang = pos * inv_ref[1:2, :]
    c64 = jnp.cos(ang)
    s64 = jnp.sin(ang) * sgn_ref[1:2, :]

    def rot128(x):
        return x * c128 + pltpu.roll(x, 64, 1) * s128

    def rot64(x):
        swapped = jnp.where(lane % 64 < 32, pltpu.roll(x, 96, 1), pltpu.roll(x, 32, 1))
        return x * c64 + swapped * s64

    scale = (DSA_HEAD_DIM ** -0.5) * math.log2(math.e)
    for h in range(DSA_HEADS):
        qb_ref[h] = (rot128(dq_ref[:, h * LANES:(h + 1) * LANES]) * scale).astype(BF16)
    kb_ref[...] = rot128(dkv_ref[:, 0:LANES]).astype(BF16)
    vb_ref[:, 0:LANES] = dkv_ref[:, LANES:2 * LANES].astype(BF16)
    vb_ref[:, LANES:2 * LANES] = jnp.ones((pos.shape[0], LANES), BF16)
    for h in range(IDX_HEADS // 2):
        qi_ref[h] = rot64(iq_ref[:, h * LANES:(h + 1) * LANES]).astype(BF16)
    ki = jnp.where(lane < IDX_DIM, rot64(tail_ref[...]), 0.0)
    kia_ref[...] = ki.astype(BF16)
    kib_ref[...] = pltpu.roll(ki, 64, 1).astype(BF16)


def _rope(proj, positions, tm):
    S = proj.shape[0]
    tm = min(tm, S)
    inv128 = ROPE_THETA ** (-jnp.arange(0, DSA_HEAD_DIM, 2, dtype=F32) / DSA_HEAD_DIM)
    inv64 = ROPE_THETA ** (-jnp.arange(0, IDX_DIM, 2, dtype=F32) / IDX_DIM)
    inv = jnp.stack([jnp.tile(inv128, 2), jnp.tile(inv64, 4)])
    sgn = jnp.asarray(np.stack([np.repeat([-1.0, 1.0], 64), np.tile(np.repeat([-1.0, 1.0], 32), 2)]), F32)
    nq = DSA_HEADS * DSA_HEAD_DIM
    ni = IDX_HEADS * IDX_DIM
    outs = (jax.ShapeDtypeStruct((DSA_HEADS, S, LANES), BF16), jax.ShapeDtypeStruct((S, LANES), BF16),
            jax.ShapeDtypeStruct((S, 2 * LANES), BF16), jax.ShapeDtypeStruct((IDX_HEADS // 2, S, LANES), BF16),
            jax.ShapeDtypeStruct((S, LANES), BF16), jax.ShapeDtypeStruct((S, LANES), BF16))
    row = lambda w: pl.BlockSpec((tm, w), lambda i: (i, 0))
    heads = lambda n: pl.BlockSpec((n, tm, LANES), lambda i: (0, i, 0))
    return pl.pallas_call(
        _rope_kernel,
        out_shape=outs,
        grid=(S // tm,),
        in_specs=[pl.BlockSpec((tm, 1), lambda i: (i, 0)),
                  pl.BlockSpec((tm, nq), lambda i: (i, COL_DQ // nq)),
                  pl.BlockSpec((tm, 2 * LANES), lambda i: (i, COL_DK // (2 * LANES))),
                  pl.BlockSpec((tm, ni), lambda i: (i, COL_IQ // ni)),
                  pl.BlockSpec((tm, LANES), lambda i: (i, COL_TAIL // LANES)),
                  pl.BlockSpec((2, LANES), lambda i: (0, 0)),
                  pl.BlockSpec((2, LANES), lambda i: (0, 0))],
        out_specs=(heads(DSA_HEADS), row(LANES), row(2 * LANES), heads(IDX_HEADS // 2), row(LANES), row(LANES)),
        compiler_params=_cparams(("arbitrary",)),
        name="rope",
    )(positions.reshape(S, 1), proj, proj, proj, proj, inv, sgn)


DSA_TQ = 256
DSA_TK = 256
_SELECT_ALL = float(np.finfo(np.float32).min)


def _dsa_kernel(topk, qi_ref, qb_ref, tail_ref, kia_ref, kib_ref, kb_ref, va_ref, o_ref,
                sc_scr, wbc_scr, smin_scr, smax_scr, m_scr, acc_scr):
    tq, tk = qb_ref.shape[1], DSA_TK
    nhp = IDX_HEADS // 2
    i = pl.program_id(0)
    t0 = i * tq
    nvis = (t0 + tq - 1) // tk + 1

    w_idx = tail_ref[:, TAIL_IW:TAIL_IW + IDX_HEADS] * ((IDX_DIM ** -0.5) * (IDX_HEADS ** -0.5))
    for h in range(IDX_HEADS):
        wbc_scr[h] = jnp.broadcast_to(w_idx[:, h:h + 1], (tq, LANES))
    smin_scr[...] = jnp.full((tq, LANES), jnp.inf, F32)
    smax_scr[...] = jnp.full((tq, LANES), -jnp.inf, F32)
    t_row = t0 + lax.broadcasted_iota(jnp.int32, (tq, LANES), 0)
    lane = lax.broadcasted_iota(jnp.int32, (tq, LANES), 1)

    def score_body(j, carry):
        off = pl.multiple_of(j * tk, tk)
        qi2 = qi_ref[...].reshape(nhp * tq, LANES)
        s_a = _dot_nt(qi2, kia_ref[pl.ds(off, tk), :])
        s_b = _dot_nt(qi2, kib_ref[pl.ds(off, tk), :])
        for c in range(tk // LANES):
            cs = slice(c * LANES, (c + 1) * LANES)
            acc = jnp.zeros((tq, LANES), F32)
            for hp in range(nhp):
                rs = slice(hp * tq, (hp + 1) * tq)
                acc = acc + jnp.maximum(s_a[rs, cs], 0.0) * wbc_scr[2 * hp]
                acc = acc + jnp.maximum(s_b[rs, cs], 0.0) * wbc_scr[2 * hp + 1]
            vis = (off + c * LANES + lane) <= t_row
            sc_scr[j, :, cs] = jnp.where(vis, acc, -jnp.inf)
            smin_scr[...] = jnp.minimum(smin_scr[...], jnp.where(vis, acc, jnp.inf))
            smax_scr[...] = jnp.maximum(smax_scr[...], jnp.where(vis, acc, -jnp.inf))
        return carry

    lax.fori_loop(0, nvis, score_body, 0)
    smin = jnp.min(smin_scr[...], axis=1, keepdims=True)
    smax = jnp.max(smax_scr[...], axis=1, keepdims=True)

    n_vis = (t0 + lax.broadcasted_iota(jnp.int32, (tq, 1), 0) + 1).astype(F32)
    select_all = n_vis <= topk
    lo0 = jnp.where(select_all, _SELECT_ALL, smin)
    hi0 = jnp.where(select_all, _SELECT_ALL, smax)

    def count_ge(x):
        x_b = jnp.broadcast_to(x, (tq, LANES))

        def body(j, c):
            for cc in range(tk // LANES):
                c = c + jnp.where(sc_scr[j, :, cc * LANES:(cc + 1) * LANES] >= x_b, 1.0, 0.0)
            return c
        c = lax.fori_loop(0, nvis, body, jnp.zeros((tq, LANES), F32))
        return jnp.sum(c, axis=1, keepdims=True)

    def bis_cond(st):
        lo, hi, c_lo = st
        active = (lo < hi) & (c_lo != topk)
        return jnp.max(jnp.where(active, 1.0, 0.0)) > 0.5

    def bis_body(st):
        lo, hi, c_lo = st
        active = (lo < hi) & (c_lo != topk)
        mid = 0.5 * lo + 0.5 * hi
        stuck = (mid <= lo) | (mid >= hi)
        probe = jnp.where(active, jnp.where(stuck, hi, mid), lo)
        c = count_ge(probe)
        ge = c >= topk
        lo_n = jnp.where(active & ge, probe, lo)
        c_n = jnp.where(active & ge, c, c_lo)
        hi_n = jnp.where(active & jnp.logical_not(ge), jnp.where(stuck, lo, mid), hi)
        return lo_n, hi_n, c_n

    thr, _, _ = lax.while_loop(bis_cond, bis_body, (lo0, hi0, n_vis))

    m_scr[...] = jnp.full(m_scr.shape, -1e30, F32)
    acc_scr[...] = jnp.zeros(acc_scr.shape, F32)
    thr_b = jnp.broadcast_to(thr, (tq, tk))

    def attn_body(j, carry):
        off = pl.multiple_of(j * tk, tk)
        bias = jnp.where(sc_scr[j] >= thr_b, 0.0, -jnp.inf)
        s = _dot_nt(qb_ref[...].reshape(DSA_HEADS * tq, LANES), kb_ref[pl.ds(off, tk), :])
        s = (s.reshape(DSA_HEADS, tq, tk) + bias[None]).reshape(DSA_HEADS * tq, tk)
        m_old = m_scr[...]
        m_new = jnp.maximum(m_old, jnp.max(s, axis=1, keepdims=True))
        p = jnp.exp2(s - jnp.concatenate([m_new] * (tk // LANES), axis=1))
        pv = jnp.dot(p.astype(BF16), va_ref[pl.ds(off, tk), :], preferred_element_type=F32)
        alpha = jnp.exp2(m_old - m_new)
        acc_scr[...] = jnp.concatenate([alpha, alpha], axis=1) * acc_scr[...] + pv
        m_scr[...] = m_new
        return carry

    lax.fori_loop(0, nvis, attn_body, 0)
    for h in range(DSA_HEADS):
        rs = slice(h * tq, (h + 1) * tq)
        o_ref[:, h * LANES:(h + 1) * LANES] = (acc_scr[rs, 0:LANES] / acc_scr[rs, LANES:2 * LANES]).astype(o_ref.dtype)


def _dsa(qi, qb, proj, kia, kib, kb, va):
    S = kb.shape[0]
    tq, tk = min(DSA_TQ, S), DSA_TK
    topk = min(DSA_TOPK, S // 4)
    nq = DSA_HEADS * DSA_HEAD_DIM
    full = lambda w: pl.BlockSpec((S, w), lambda i: (0, 0), pipeline_mode=pl.Buffered(1))
    return pl.pallas_call(
        functools.partial(_dsa_kernel, topk),
        out_shape=jax.ShapeDtypeStruct((S, nq), BF16),
        grid=(S // tq,),
        in_specs=[pl.BlockSpec((IDX_HEADS // 2, tq, LANES), lambda i: (0, i, 0)),
                  pl.BlockSpec((DSA_HEADS, tq, LANES), lambda i: (0, i, 0)),
                  pl.BlockSpec((tq, LANES), lambda i: (i, COL_TAIL // LANES)),
                  full(LANES), full(LANES), full(LANES), full(2 * LANES)],
        out_specs=pl.BlockSpec((tq, nq), lambda i: (i, 0)),
        scratch_shapes=[pltpu.VMEM((S // tk, tq, tk), F32),
                        pltpu.VMEM((IDX_HEADS, tq, LANES), F32),
                        pltpu.VMEM((tq, LANES), F32),
                        pltpu.VMEM((tq, LANES), F32),
                        pltpu.VMEM((DSA_HEADS * tq, LANES), F32),
                        pltpu.VMEM((DSA_HEADS * tq, 2 * LANES), F32)],
        compiler_params=_cparams(("arbitrary",)),
        name="dsa",
    )(qi, qb, proj, kia, kib, kb, va)


def _xattn_kernel(q_ref, kv_ref, o_ref):
    d = XATTN_HEAD_DIM
    nkv = XATTN_HEADS * d
    for h in range(XATTN_HEADS):
        q = q_ref[:, h * d:(h + 1) * d]
        k = kv_ref[:, h * d:(h + 1) * d]
        v = kv_ref[:, nkv + h * d:nkv + (h + 1) * d]
        s = _dot_nt(q, k) * (d ** -0.5)
        s = s - jnp.max(s, axis=1, keepdims=True)
        p = jnp.exp(s)
        p = p / jnp.sum(p, axis=1, keepdims=True)
        o_ref[:, h * d:(h + 1) * d] = jnp.dot(p.astype(BF16), v, preferred_element_type=F32).astype(o_ref.dtype)


def _xattn(q, kv, tm):
    S, N = q.shape
    M = kv.shape[0]
    tm = min(tm, S)
    return pl.pallas_call(
        _xattn_kernel,
        out_shape=jax.ShapeDtypeStruct((S, N), BF16),
        grid=(S // tm,),
        in_specs=[pl.BlockSpec((tm, N), lambda i: (i, 0)),
                  pl.BlockSpec((M, 2 * N), lambda i: (0, 0))],
        out_specs=pl.BlockSpec((tm, N), lambda i: (i, 0)),
        compiler_params=_cparams(("arbitrary",)),
        name="xattn",
    )(q, kv)


PEER_TS = 256


def _top16_distinct(s):
    vals, cnts = [], []
    rem = s
    for _ in range(PEER_TOPK):
        m = jnp.max(rem, axis=0, keepdims=True)
        eq = rem == m
        cnts.append(jnp.sum(jnp.where(eq, 1.0, 0.0), axis=0, keepdims=True))
        vals.append(m)
        rem = jnp.where(eq, -jnp.inf, rem)
    return vals, cnts


def _stack_rows(rows):
    n, T = len(rows), rows[0].shape[1]
    idx = lax.broadcasted_iota(jnp.int32, (n, T), 0)
    out = jnp.broadcast_to(rows[0], (n, T))
    for r in range(1, n):
        out = jnp.where(idx == r, rows[r], out)
    return out


def _peer_select_kernel(q_ref, sk_ref, c_ref, s2_ref, e1_ref, e2_ref):
    T = q_ref.shape[0]
    for h in range(PEER_HEADS):
        s1 = _dot_nt(sk_ref[h, 0], q_ref[:, (2 * h) * PEER_HALF:(2 * h + 1) * PEER_HALF])
        s2 = _dot_nt(sk_ref[h, 1], q_ref[:, (2 * h + 1) * PEER_HALF:(2 * h + 2) * PEER_HALF])
        a_vals, a_cnts = _top16_distinct(s1)
        b_vals, b_cnts = _top16_distinct(s2)
        a_mat, ac_mat = _stack_rows(a_vals), _stack_rows(a_cnts)
        b_mat, bc_mat = _stack_rows(b_vals), _stack_rows(b_cnts)
        k = PEER_TOPK
        row8 = lax.broadcasted_iota(jnp.int32, (8, T), 0)
        row16 = lax.broadcasted_iota(jnp.int32, (k, T), 0)
        ninf = -jnp.inf
        pieces = [
            (a_vals[0] + b_mat, a_cnts[0] * bc_mat),
            (a_vals[1] + b_mat[0:8], a_cnts[1] * bc_mat[0:8]),
            (jnp.where(row8 < k // 3, a_vals[2] + b_mat[0:8], ninf), a_cnts[2] * bc_mat[0:8]),
            (jnp.where(row8 < k // 4, a_vals[3] + b_mat[0:8], ninf), a_cnts[3] * bc_mat[0:8]),
            (jnp.where(row16 >= 4, b_vals[0] + a_mat, ninf), b_cnts[0] * ac_mat),
            (jnp.where(row8 >= 4, b_vals[1] + a_mat[0:8], ninf), b_cnts[1] * ac_mat[0:8]),
            (jnp.where(row8 == 4, b_vals[2] + a_mat[0:8], ninf), b_cnts[2] * ac_mat[0:8]),
        ]
        cand = jnp.concatenate([p[0] for p in pieces], axis=0)
        wgt = jnp.concatenate([p[1] for p in pieces], axis=0)
        rem = cand
        n = jnp.zeros((1, T), F32)
        thr = jnp.full((1, T), -jnp.inf, F32)
        for _ in range(PEER_TOPK):
            m = jnp.max(rem, axis=0, keepdims=True)
            eq = rem == m
            thr = jnp.where(n < PEER_TOPK, m, thr)
            n = n + jnp.sum(jnp.where(eq, wgt, 0.0), axis=0, keepdims=True)
            rem = jnp.where(eq, -jnp.inf, rem)
        vmax = a_vals[0] + b_vals[0]
        keep = cand >= thr
        z = jnp.sum(jnp.where(keep, wgt * jnp.exp(cand - vmax), 0.0), axis=0, keepdims=True)
        inf = jnp.inf
        c_rows = [jnp.min(jnp.where(keep[0:16], b_mat, inf), axis=0, keepdims=True),
                  jnp.min(jnp.where(keep[16:24], b_mat[0:8], inf), axis=0, keepdims=True),
                  jnp.min(jnp.where(keep[24:32], b_mat[0:8], inf), axis=0, keepdims=True),
                  jnp.min(jnp.where(keep[32:40], b_mat[0:8], inf), axis=0, keepdims=True)]
        c_hi = jnp.where(keep[40:56], b_vals[0], inf)
        c_hi = jnp.minimum(c_hi, jnp.concatenate(
            [jnp.minimum(jnp.where(keep[56:64], b_vals[1], inf), jnp.where(keep[64:72], b_vals[2], inf)),
             jnp.full((8, T), inf, F32)], axis=0))
        c = jnp.full(s1.shape, inf, F32)
        for r in range(PEER_TOPK):
            c_r = c_rows[r] if r < 4 else c_hi[r:r + 1]
            c = jnp.where(s1 == a_vals[r], c_r, c)
        c_ref[h] = c
        s2_ref[h] = s2
        e1_ref[h] = jnp.exp(s1 - a_vals[0])
        e2_ref[h] = jnp.exp(s2 - b_vals[0]) / z


def _peer_select(q, sub_keys):
    S = q.shape[0]
    ts = min(PEER_TS, S)
    big = jax.ShapeDtypeStruct((PEER_HEADS, PEER_NKEYS, S), F32)
    bspec = pl.BlockSpec((PEER_HEADS, PEER_NKEYS, ts), lambda i: (0, 0, i))
    return pl.pallas_call(
        _peer_select_kernel,
        out_shape=(big, big, big, big),
        grid=(S // ts,),
        in_specs=[pl.BlockSpec((ts, q.shape[1]), lambda i: (i, 0)),
                  pl.BlockSpec(sub_keys.shape, lambda i: (0, 0, 0, 0))],
        out_specs=(bspec, bspec, bspec, bspec),
        compiler_params=_cparams(("arbitrary",)),
        name="peer_select",
    )(q, sub_keys)


PEER_TM = 512
PEER_TE = 1024


def _gelu_tanh_scaled(x, g):
    inner = x * (0.7978845608028654 + 0.035677408136300125 * (x * x))
    return (g * (0.5 * x)) * (1.0 + jnp.tanh(inner))


def _zero_after(x):
    bits = lax.bitcast_convert_type(x, jnp.uint32)
    z = lax.shift_right_logical(lax.shift_right_logical(bits, jnp.uint32(16)), jnp.uint32(16))
    return z[0:1, :].astype(F32)


def _peer_dense_kernel(xn_ref, u_ref, v_ref, c_ref, s2_ref, e1_ref, e2_ref, o_ref, wt_scr):
    e = pl.program_id(1)
    te, tm = u_ref.shape[0], xn_ref.shape[0]
    nk = PEER_NKEYS
    nr = te // nk
    half = te // 2
    tok_chunk = 2 * LANES
    order = None
    for tc in range(tm // tok_chunk):
        ts = slice(tc * tok_chunk, (tc + 1) * tok_chunk)
        for hf in range(2):
            act = _dot_nt(u_ref[hf * half:(hf + 1) * half, :], xn_ref[ts, :])
            for rr in range(nr // 2):
                r = hf * (nr // 2) + rr
                c_rows = [c_ref[h, pl.ds(e * nr + r, 1), :] for h in range(PEER_HEADS)]
                e1_rows = [e1_ref[h, pl.ds(e * nr + r, 1), :] for h in range(PEER_HEADS)]
                for tb in range(tok_chunk // LANES):
                    ls = slice(tc * tok_chunk + tb * LANES, tc * tok_chunk + (tb + 1) * LANES)
                    g = jnp.zeros((nk, LANES), F32)
                    for h in range(PEER_HEADS):
                        c_row = c_rows[h][:, ls]
                        if order is not None:
                            c_row = c_row + order
                        g = g + jnp.where(s2_ref[h, :, ls] >= c_row, e2_ref[h, :, ls], 0.0) * e1_rows[h][:, ls]
                    w = _gelu_tanh_scaled(act[rr * nk:(rr + 1) * nk, tb * LANES:(tb + 1) * LANES], g)
                    wt_scr[ls, r * nk:(r + 1) * nk] = w.T.astype(BF16)
            order = _zero_after(w[0:8, :])
        contrib = jnp.dot(wt_scr[ts, :], v_ref[...], preferred_element_type=F32)
        prev = jnp.where(e == 0, 0.0, o_ref[ts, :])
        o_ref[ts, :] = prev + contrib


def _peer_dense(xn, u, v, c, s2, e1, e2):
    S, D = xn.shape
    E = u.shape[0]
    tm, te = min(PEER_TM, S), PEER_TE
    fac = pl.BlockSpec((PEER_HEADS, PEER_NKEYS, tm), lambda i, e: (0, 0, i))
    return pl.pallas_call(
        _peer_dense_kernel,
        out_shape=jax.ShapeDtypeStruct((S, D), F32),
        grid=(S // tm, E // te),
        in_specs=[pl.BlockSpec((tm, D), lambda i, e: (i, 0)),
                  pl.BlockSpec((te, D), lambda i, e: (e, 0)),
                  pl.BlockSpec((te, D), lambda i, e: (e, 0)),
                  fac, fac, fac, fac],
        out_specs=pl.BlockSpec((tm, D), lambda i, e: (i, 0)),
        scratch_shapes=[pltpu.VMEM((tm, te), BF16)],
        compiler_params=_cparams(("arbitrary", "arbitrary")),
        name="peer_dense",
    )(xn, u, v, c, s2, e1, e2)


def _final_kernel(h_ref, y_ref, g_ref, o_ref):
    x = h_ref[...] + y_ref[...]
    o_ref[...] = x * lax.rsqrt(jnp.mean(x * x, axis=-1, keepdims=True) + RMS_EPS) * g_ref[...]


def _final(h, y, g, tm):
    S, D = h.shape
    tm = min(tm, S)
    spec = pl.BlockSpec((tm, D), lambda i: (i, 0))
    return pl.pallas_call(
        _final_kernel,
        out_shape=jax.ShapeDtypeStruct((S, D), F32),
        grid=(S // tm,),
        in_specs=[spec, spec, pl.BlockSpec((1, D), lambda i: (0, 0))],
        out_specs=spec,
        compiler_params=_cparams(("arbitrary",)),
        name="final",
    )(h, y, g.reshape(1, D))


def _permute_w_in(w_in, l):
    c = _SRC
    sl = lambda a, b: w_in[l, :, c[a]:c[b]]
    pad = jnp.zeros((w_in.shape[1], NP_COLS - COL_TAIL - 96), w_in.dtype)
    cols = [sl("gv", "gr"), sl("gr", "glow"), sl("dq", "dk"), sl("ga", "gb"), sl("gb", "end"),
            sl("gq", "gk"), sl("gk", "gv"), sl("iq", "ik"), sl("dk", "dv"), sl("dv", "iq"),
            sl("ik", "iw"), sl("glow", "dq"), sl("iw", "ga"), pad]
    return jnp.concatenate(cols, axis=1).astype(BF16)


def _layer(h, mem, positions, norm_mix_g, w_in_perm, gla_gate_w2, gla_gate_b, gla_head_norm_g,
           w_proj_gla, w_proj_dsa, w_out, norm_x_g, norm_mem_g, w_xq, w_xk, w_xv, w_xo,
           norm_ffn_g, peer_wq, peer_sub_keys, peer_u, peer_v):
    (proj,) = _norm_mm(h, norm_mix_g, w_in_perm, F32, 1024, 1536, "in_proj")
    w2p = jnp.zeros((LANES, GLA_HEADS * GLA_DK), F32).at[TAIL_GLOW:TAIL_GLOW + GLA_GATE_RANK].set(gla_gate_w2)
    o_a = _gla(proj, w2p, gla_gate_b, gla_head_norm_g)
    qb, kb, vb, qi, kia, kib = _rope(proj, positions, 512)
    o_b = _dsa(qi, qb, proj, kia, kib, kb, vb)
    merged = _merge(o_a, o_b, w_proj_gla.astype(BF16), w_proj_dsa.astype(BF16), proj, 512, 2048)
    h = _mm_res(merged, w_out.astype(BF16), h, 512, 2048, "out_proj")

    (q_x,) = _norm_mm(h, norm_x_g, w_xq.astype(BF16), BF16, 512, 2048, "xattn_q")
    (kv_x,) = _norm_mm(mem, norm_mem_g, jnp.concatenate([w_xk, w_xv], axis=1).astype(BF16), BF16, 1024, 512,
                       "xattn_kv")
    o_x = _xattn(q_x, kv_x, 512)
    h = _mm_res(o_x, w_xo.astype(BF16), h, 512, 2048, "xattn_o")

    q_p, xn = _norm_mm(h, norm_ffn_g, peer_wq.astype(BF16), BF16, 512, 2048, "peer_q", keep_xn=True)
    c, s2, e1, e2 = _peer_select(q_p, peer_sub_keys.astype(BF16))
    y = _peer_dense(xn, peer_u.astype(BF16), peer_v.astype(BF16), c, s2, e1, e2)
    return h, y


def _take(a, i):
    return a.reshape(a.shape[1:]) if a.shape[0] == 1 else a[i]


def kernel(x, mem, positions, norm_mix_g, w_in, gla_gate_w2, gla_gate_b, gla_head_norm_g, w_proj_gla, w_proj_dsa, w_out, norm_x_g, norm_mem_g, w_xq, w_xk, w_xv, w_xo, norm_ffn_g, peer_wq, peer_sub_keys, peer_u, peer_v, norm_final_g):
    B, S, D = x.shape
    depth = w_in.shape[0]
    outs = []
    for b in range(B):
        h = _take(x, b)
        y = None
        for l in range(depth):
            if y is not None:
                h = h + y
            h, y = _layer(h, _take(mem, b), _take(positions, b), _take(norm_mix_g, l), _permute_w_in(w_in, l),
                          _take(gla_gate_w2, l), _take(gla_gate_b, l), _take(gla_head_norm_g, l),
                          _take(w_proj_gla, l), _take(w_proj_dsa, l), _take(w_out, l), _take(norm_x_g, l),
                          _take(norm_mem_g, l), _take(w_xq, l), _take(w_xk, l), _take(w_xv, l), _take(w_xo, l),
                          _take(norm_ffn_g, l), _take(peer_wq, l), _take(peer_sub_keys, l), _take(peer_u, l),
                          _take(peer_v, l))
        outs.append(_final(h, y, norm_final_g, 512))
    return jnp.stack(outs)
```

```python
import functools
import math

import numpy as np
import jax
import jax.numpy as jnp
from jax import lax
from jax.experimental import pallas as pl
from jax.experimental.pallas import tpu as pltpu

F32 = jnp.float32
BF16 = jnp.bfloat16

D_MODEL = 2048
GLA_HEADS = 4
GLA_DK = 256
GLA_DV = 512
GLA_GATE_RANK = 16
GLA_GATE_TEMP = 16.0
DSA_HEADS = 16
DSA_HEAD_DIM = 128
IDX_HEADS = 16
IDX_DIM = 64
DSA_TOPK = 256
XATTN_HEADS = 4
XATTN_HEAD_DIM = D_MODEL // XATTN_HEADS
PEER_HEADS = 8
PEER_NKEYS = 128
PEER_HALF = 128
PEER_TOPK = 16
ROPE_THETA = 10000.0
RMS_EPS = 1e-6

LANES = 128
V7X_VMEM_LIMIT_BYTES = 56 * 1024 * 1024

COL_GV = 0
COL_GR = 2048
COL_DQ = 4096
COL_GA = 6144
COL_GB = 8192
COL_GQ = 10240
COL_GK = 11264
COL_IQ = 12288
COL_DK = 13312
COL_DV = 13440
COL_TAIL = 13568
TAIL_GLOW = 64
TAIL_IW = 80
NP_COLS = 13824

_SRC = dict(gq=0, gk=1024, gv=2048, gr=4096, glow=6144, dq=6160, dk=8208, dv=8336,
            iq=8464, ik=9488, iw=9552, ga=9568, gb=11616, end=13664)


def _cparams(sem, vmem=V7X_VMEM_LIMIT_BYTES):
    return pltpu.CompilerParams(dimension_semantics=sem, vmem_limit_bytes=vmem)


def _weight_spec(shape, index_map, resident):
    if resident:
        return pl.BlockSpec(shape, index_map, pipeline_mode=pl.Buffered(1))
    return pl.BlockSpec(shape, index_map)


def _dot_nt(a, b):
    return lax.dot_general(a, b, (((1,), (1,)), ((), ())), preferred_element_type=F32)


def _split3(x):
    hi = x.astype(BF16)
    r1 = x - hi.astype(F32)
    mid = r1.astype(BF16)
    lo = (r1 - mid.astype(F32)).astype(BF16)
    return hi, mid, lo


def _norm_mm_kernel(keep_xn, x_ref, g_ref, w_ref, o_ref, *rest):
    a_scr = rest[-1]

    @pl.when(pl.program_id(1) == 0)
    def _():
        x = x_ref[...]
        y = x * lax.rsqrt(jnp.mean(x * x, axis=-1, keepdims=True) + RMS_EPS) * g_ref[...]
        a_scr[...] = y.astype(BF16)
        if keep_xn:
            rest[0][...] = a_scr[...]

    o_ref[...] = jnp.dot(a_scr[...], w_ref[...], preferred_element_type=F32).astype(o_ref.dtype)


def _norm_mm(x, g, w, out_dtype, tm, tn, name, keep_xn=False):
    M, K = x.shape
    N = w.shape[1]
    tm, tn = min(tm, M), min(tn, N)
    out_shape = [jax.ShapeDtypeStruct((M, N), out_dtype)]
    out_specs = [pl.BlockSpec((tm, tn), lambda i, j: (i, j))]
    if keep_xn:
        out_shape.append(jax.ShapeDtypeStruct((M, K), BF16))
        out_specs.append(pl.BlockSpec((tm, K), lambda i, j: (i, 0)))
    return pl.pallas_call(
        functools.partial(_norm_mm_kernel, keep_xn),
        out_shape=tuple(out_shape),
        grid=(M // tm, N // tn),
        in_specs=[pl.BlockSpec((tm, K), lambda i, j: (i, 0)),
                  pl.BlockSpec((1, K), lambda i, j: (0, 0)),
                  _weight_spec((K, tn), lambda i, j: (0, j), tn == N)],
        out_specs=tuple(out_specs),
        scratch_shapes=[pltpu.VMEM((tm, K), BF16)],
        compiler_params=_cparams(("arbitrary", "arbitrary")),
        name=name,
    )(x, g.reshape(1, K), w)


def _mm_res_kernel(a_ref, w_ref, r_ref, o_ref):
    o_ref[...] = r_ref[...] + jnp.dot(a_ref[...], w_ref[...], preferred_element_type=F32)


def _mm_res(a, w, res, tm, tn, name):
    M, K = a.shape
    N = w.shape[1]
    tm, tn = min(tm, M), min(tn, N)
    return pl.pallas_call(
        _mm_res_kernel,
        out_shape=jax.ShapeDtypeStruct((M, N), F32),
        grid=(M // tm, N // tn),
        in_specs=[pl.BlockSpec((tm, K), lambda i, j: (i, 0)),
                  _weight_spec((K, tn), lambda i, j: (0, j), tn == N),
                  pl.BlockSpec((tm, tn), lambda i, j: (i, j))],
        out_specs=pl.BlockSpec((tm, tn), lambda i, j: (i, j)),
        compiler_params=_cparams(("arbitrary", "arbitrary")),
        name=name,
    )(a, w, res)


def _merge_kernel(oa_ref, ob_ref, wa_ref, wb_ref, ga_ref, gb_ref, o_ref):
    ya = jnp.dot(oa_ref[...], wa_ref[...], preferred_element_type=F32)
    yb = jnp.dot(ob_ref[...], wb_ref[...], preferred_element_type=F32)
    o_ref[...] = (jax.nn.sigmoid(ga_ref[...]) * ya + jax.nn.sigmoid(gb_ref[...]) * yb).astype(o_ref.dtype)


def _merge(oa, ob, wa, wb, proj, tm, tn):
    M, K = oa.shape
    N = wa.shape[1]
    tm, tn = min(tm, M), min(tn, N)
    ca, cb = COL_GA // tn, COL_GB // tn
    return pl.pallas_call(
        _merge_kernel,
        out_shape=jax.ShapeDtypeStruct((M, N), BF16),
        grid=(M // tm, N // tn),
        in_specs=[pl.BlockSpec((tm, K), lambda i, j: (i, 0)),
                  pl.BlockSpec((tm, K), lambda i, j: (i, 0)),
                  _weight_spec((K, tn), lambda i, j: (0, j), tn == N),
                  _weight_spec((K, tn), lambda i, j: (0, j), tn == N),
                  pl.BlockSpec((tm, tn), lambda i, j: (i, ca + j)),
                  pl.BlockSpec((tm, tn), lambda i, j: (i, cb + j))],
        out_specs=pl.BlockSpec((tm, tn), lambda i, j: (i, j)),
        compiler_params=_cparams(("arbitrary", "arbitrary")),
        name="merge",
    )(oa, ob, wa, wb, proj, proj)


GLA_C = 256
GLA_LEVELS = (128, 64, 32, 16)
GLA_DIAG = 16


def _gla_matrices(C):
    t = np.arange(C)[:, None]
    u = np.arange(C)[None, :]
    L = (u <= t).astype(np.float32)
    mats = [L, (u > t).astype(np.float32)]
    for m in GLA_LEVELS:
        anchor = (t // (2 * m)) * (2 * m) + m - 1
        mats.append(L - (u <= anchor).astype(np.float32))
    anchor = (t // GLA_DIAG) * GLA_DIAG
    mats.append(L - (u <= anchor).astype(np.float32))
    return np.concatenate(mats, axis=0)


GLA_HPB = 4


def _gla_kernel(q_ref, k_ref, v_ref, r_ref, tail_ref, w2_ref, gb_ref, hg_ref, mats_ref, o_ref, st_scr):
    C = GLA_C

    @pl.when(pl.program_id(1) == 0)
    def _():
        st_scr[...] = jnp.zeros_like(st_scr)

    t_hi, t_mid, _ = _split3(tail_ref[...])
    mats = mats_ref[...]
    row = lax.broadcasted_iota(jnp.int32, (C, C), 0)
    col = lax.broadcasted_iota(jnp.int32, (C, C), 1)
    masks = []
    for m in GLA_LEVELS:
        sh = int(math.log2(2 * m))
        masks.append(((row >> sh) == (col >> sh)) & ((row & m) != 0) & ((col & m) == 0))
    sh = int(math.log2(GLA_DIAG))
    masks.append(((row >> sh) == (col >> sh)) & (col <= row))

    for hh in range(GLA_HPB):
        ks = slice(hh * GLA_DK, (hh + 1) * GLA_DK)
        vs = slice(hh * GLA_DV, (hh + 1) * GLA_DV)
        w_hi, w_mid, _ = _split3(w2_ref[:, ks])
        z = (jnp.dot(t_hi, w_hi, preferred_element_type=F32) + jnp.dot(t_hi, w_mid, preferred_element_type=F32)
             + jnp.dot(t_mid, w_hi, preferred_element_type=F32)) + gb_ref[:, ks]
        log_a = (jnp.minimum(z, 0.0) - jnp.log1p(jnp.exp(-jnp.abs(z)))) * (1.0 / GLA_GATE_TEMP)

        a_hi, a_mid, a_lo = _split3(log_a)
        dec = (jnp.dot(mats, a_hi, preferred_element_type=F32) + jnp.dot(mats, a_mid, preferred_element_type=F32)
               + jnp.dot(mats, a_lo, preferred_element_type=F32))
        b = dec[0:C]
        b_rest = dec[C:2 * C]

        q = q_ref[:, ks] * (GLA_DK ** -0.5)
        k = k_ref[:, ks]
        v = v_ref[:, vs]
        v_bf = v.astype(BF16)

        attn = jnp.zeros((C, C), F32)
        for li in range(len(GLA_LEVELS) + 1):
            d = dec[(2 + li) * C:(3 + li) * C]
            if li < len(GLA_LEVELS):
                qs = (q * jnp.exp(jnp.minimum(d, 0.0))).astype(BF16)
                ks_ = (k * jnp.exp(jnp.minimum(-d, 0.0))).astype(BF16)
            else:
                qs = (q * jnp.exp(d)).astype(BF16)
                ks_ = (k * jnp.exp(-d)).astype(BF16)
            attn = attn + jnp.where(masks[li], _dot_nt(qs, ks_), 0.0)

        st = st_scr[hh]
        o = _dot_nt((q * jnp.exp(b)).astype(BF16), st.astype(BF16))
        o = o + jnp.dot(attn.astype(BF16), v_bf, preferred_element_type=F32)

        kd = (k * jnp.exp(b_rest)).astype(BF16)
        upd = jnp.dot(v.T.astype(BF16), kd, preferred_element_type=F32)
        st_scr[hh] = st * jnp.exp(b[C - 1:C, :]) + upd

        y = o * lax.rsqrt(jnp.mean(o * o, axis=-1, keepdims=True) + RMS_EPS) * hg_ref[:, vs]
        r = r_ref[:, vs]
        o_ref[:, vs] = (y * (r * jax.nn.sigmoid(r))).astype(o_ref.dtype)


def _gla(proj, w2p, gate_b, head_g):
    S = proj.shape[0]
    C = GLA_C
    G = GLA_HPB
    mats = jnp.asarray(_gla_matrices(C), dtype=BF16)
    nm = mats.shape[0]
    wk, wv = G * GLA_DK, G * GLA_DV
    cq, ck = COL_GQ // wk, COL_GK // wk
    cv, cr = COL_GV // wv, COL_GR // wv
    ct = COL_TAIL // LANES
    return pl.pallas_call(
        _gla_kernel,
        out_shape=jax.ShapeDtypeStruct((S, GLA_HEADS * GLA_DV), BF16),
        grid=(GLA_HEADS // G, S // C),
        in_specs=[pl.BlockSpec((C, wk), lambda h, c: (c, cq + h)),
                  pl.BlockSpec((C, wk), lambda h, c: (c, ck + h)),
                  pl.BlockSpec((C, wv), lambda h, c: (c, cv + h)),
                  pl.BlockSpec((C, wv), lambda h, c: (c, cr + h)),
                  pl.BlockSpec((C, LANES), lambda h, c: (c, ct)),
                  pl.BlockSpec((LANES, wk), lambda h, c: (0, h)),
                  pl.BlockSpec((1, wk), lambda h, c: (0, h)),
                  pl.BlockSpec((1, wv), lambda h, c: (0, h)),
                  pl.BlockSpec((nm, C), lambda h, c: (0, 0))],
        out_specs=pl.BlockSpec((C, wv), lambda h, c: (c, h)),
        scratch_shapes=[pltpu.VMEM((G, GLA_DV, GLA_DK), F32)],
        compiler_params=_cparams(("arbitrary", "arbitrary")),
        name="gla",
    )(proj, proj, proj, proj, proj, w2p, gate_b.reshape(1, -1), head_g.reshape(1, -1), mats)


def _rope_kernel(pos_ref, dq_ref, dkv_ref, iq_ref, tail_ref, inv_ref, sgn_ref,
                 qb_ref, kb_ref, vb_ref, qi_ref, kia_ref, kib_ref):
    pos = pos_ref[...].astype(F32)
    lane = lax.broadcasted_iota(jnp.int32, (pos.shape[0], LANES), 1)

    ang = pos * inv_ref[0:1, :]
    c128 = jnp.cos(ang)
    s128 = jnp.sin(ang) * sgn_ref[0:1, :]
    ang = pos * inv_ref[1:2, :]
    c64 = jnp.cos(ang)
    s64 = jnp.sin(ang) * sgn_ref[1:2, :]

    def rot128(x):
        return x * c128 + pltpu.roll(x, 64, 1) * s128

    def rot64(x):
        swapped = jnp.where(lane % 64 < 32, pltpu.roll(x, 96, 1), pltpu.roll(x, 32, 1))
        return x * c64 + swapped * s64

    scale = (DSA_HEAD_DIM ** -0.5) * math.log2(math.e)
    for h in range(DSA_HEADS):
        qb_ref[h] = (rot128(dq_ref[:, h * LANES:(h + 1) * LANES]) * scale).astype(BF16)
    kb_ref[...] = rot128(dkv_ref[:, 0:LANES]).astype(BF16)
    vb_ref[:, 0:LANES] = dkv_ref[:, LANES:2 * LANES].astype(BF16)
    vb_ref[:, LANES:2 * LANES] = jnp.ones((pos.shape[0], LANES), BF16)
    for h in range(IDX_HEADS // 2):
        qi_ref[h] = rot64(iq_ref[:, h * LANES:(h + 1) * LANES]).astype(BF16)
    ki = jnp.where(lane < IDX_DIM, rot64(tail_ref[...]), 0.0)
    kia_ref[...] = ki.astype(BF16)
    kib_ref[...] = pltpu.roll(ki, 64, 1).astype(BF16)


def _rope(proj, positions, tm):
    S = proj.shape[0]
    tm = min(tm, S)
    inv128 = ROPE_THETA ** (-jnp.arange(0, DSA_HEAD_DIM, 2, dtype=F32) / DSA_HEAD_DIM)
    inv64 = ROPE_THETA ** (-jnp.arange(0, IDX_DIM, 2, dtype=F32) / IDX_DIM)
    inv = jnp.stack([jnp.tile(inv128, 2), jnp.tile(inv64, 4)])
    sgn = jnp.asarray(np.stack([np.repeat([-1.0, 1.0], 64), np.tile(np.repeat([-1.0, 1.0], 32), 2)]), F32)
    nq = DSA_HEADS * DSA_HEAD_DIM
    ni = IDX_HEADS * IDX_DIM
    outs = (jax.ShapeDtypeStruct((DSA_HEADS, S, LANES), BF16), jax.ShapeDtypeStruct((S, LANES), BF16),
            jax.ShapeDtypeStruct((S, 2 * LANES), BF16), jax.ShapeDtypeStruct((IDX_HEADS // 2, S, LANES), BF16),
            jax.ShapeDtypeStruct((S, LANES), BF16), jax.ShapeDtypeStruct((S, LANES), BF16))
    row = lambda w: pl.BlockSpec((tm, w), lambda i: (i, 0))
    heads = lambda n: pl.BlockSpec((n, tm, LANES), lambda i: (0, i, 0))
    return pl.pallas_call(
        _rope_kernel,
        out_shape=outs,
        grid=(S // tm,),
        in_specs=[pl.BlockSpec((tm, 1), lambda i: (i, 0)),
                  pl.BlockSpec((tm, nq), lambda i: (i, COL_DQ // nq)),
                  pl.BlockSpec((tm, 2 * LANES), lambda i: (i, COL_DK // (2 * LANES))),
                  pl.BlockSpec((tm, ni), lambda i: (i, COL_IQ // ni)),
                  pl.BlockSpec((tm, LANES), lambda i: (i, COL_TAIL // LANES)),
                  pl.BlockSpec((2, LANES), lambda i: (0, 0)),
                  pl.BlockSpec((2, LANES), lambda i: (0, 0))],
        out_specs=(heads(DSA_HEADS), row(LANES), row(2 * LANES), heads(IDX_HEADS // 2), row(LANES), row(LANES)),
        compiler_params=_cparams(("arbitrary",)),
        name="rope",
    )(positions.reshape(S, 1), proj, proj, proj, proj, inv, sgn)


DSA_TQ = 256
DSA_TK = 256
_SELECT_ALL = float(np.finfo(np.float32).min)


def _dsa_kernel(topk, qi_ref, qb_ref, tail_ref, kia_ref, kib_ref, kb_ref, va_ref, o_ref,
                sc_scr, wbc_scr, smin_scr, smax_scr, m_scr, acc_scr):
    tq, tk = qb_ref.shape[1], DSA_TK
    nhp = IDX_HEADS // 2
    i = pl.program_id(0)
    t0 = i * tq
    nvis = (t0 + tq - 1) // tk + 1

    w_idx = tail_ref[:, TAIL_IW:TAIL_IW + IDX_HEADS] * ((IDX_DIM ** -0.5) * (IDX_HEADS ** -0.5))
    for h in range(IDX_HEADS):
        wbc_scr[h] = jnp.broadcast_to(w_idx[:, h:h + 1], (tq, LANES))
    smin_scr[...] = jnp.full((tq, LANES), jnp.inf, F32)
    smax_scr[...] = jnp.full((tq, LANES), -jnp.inf, F32)
    t_row = t0 + lax.broadcasted_iota(jnp.int32, (tq, LANES), 0)
    lane = lax.broadcasted_iota(jnp.int32, (tq, LANES), 1)

    def score_body(j, carry):
        off = pl.multiple_of(j * tk, tk)
        qi2 = qi_ref[...].reshape(nhp * tq, LANES)
        s_a = _dot_nt(qi2, kia_ref[pl.ds(off, tk), :])
        s_b = _dot_nt(qi2, kib_ref[pl.ds(off, tk), :])
        for c in range(tk // LANES):
            cs = slice(c * LANES, (c + 1) * LANES)
            acc = jnp.zeros((tq, LANES), F32)
            for hp in range(nhp):
                rs = slice(hp * tq, (hp + 1) * tq)
                acc = acc + jnp.maximum(s_a[rs, cs], 0.0) * wbc_scr[2 * hp]
                acc = acc + jnp.maximum(s_b[rs, cs], 0.0) * wbc_scr[2 * hp + 1]
            vis = (off + c * LANES + lane) <= t_row
            sc_scr[j, :, cs] = jnp.where(vis, acc, -jnp.inf)
            smin_scr[...] = jnp.minimum(smin_scr[...], jnp.where(vis, acc, jnp.inf))
            smax_scr[...] = jnp.maximum(smax_scr[...], jnp.where(vis, acc, -jnp.inf))
        return carry

    lax.fori_loop(0, nvis, score_body, 0)
    smin = jnp.min(smin_scr[...], axis=1, keepdims=True)
    smax = jnp.max(smax_scr[...], axis=1, keepdims=True)

    n_vis = (t0 + lax.broadcasted_iota(jnp.int32, (tq, 1), 0) + 1).astype(F32)
    select_all = n_vis <= topk
    lo0 = jnp.where(select_all, _SELECT_ALL, smin)
    hi0 = jnp.where(select_all, _SELECT_ALL, smax)

    def count_ge(x):
        x_b = jnp.broadcast_to(x, (tq, LANES))

        def body(j, c):
            for cc in range(tk // LANES):
                c = c + jnp.where(sc_scr[j, :, cc * LANES:(cc + 1) * LANES] >= x_b, 1.0, 0.0)
            return c
        c = lax.fori_loop(0, nvis, body, jnp.zeros((tq, LANES), F32))
        return jnp.sum(c, axis=1, keepdims=True)

    def bis_cond(st):
        lo, hi, c_lo = st
        active = (lo < hi) & (c_lo != topk)
        return jnp.max(jnp.where(active, 1.0, 0.0)) > 0.5

    def bis_body(st):
        lo, hi, c_lo = st
        active = (lo < hi) & (c_lo != topk)
        mid = 0.5 * lo + 0.5 * hi
        stuck = (mid <= lo) | (mid >= hi)
        probe = jnp.where(active, jnp.where(stuck, hi, mid), lo)
        c = count_ge(probe)
        ge = c >= topk
        lo_n = jnp.where(active & ge, probe, lo)
        c_n = jnp.where(active & ge, c, c_lo)
        hi_n = jnp.where(active & jnp.logical_not(ge), jnp.where(stuck, lo, mid), hi)
        return lo_n, hi_n, c_n

    thr, _, _ = lax.while_loop(bis_cond, bis_body, (lo0, hi0, n_vis))

    m_scr[...] = jnp.full(m_scr.shape, -1e30, F32)
    acc_scr[...] = jnp.zeros(acc_scr.shape, F32)
    thr_b = jnp.broadcast_to(thr, (tq, tk))

    def attn_body(j, carry):
        off = pl.multiple_of(j * tk, tk)
        bias = jnp.where(sc_scr[j] >= thr_b, 0.0, -jnp.inf)
        s = _dot_nt(qb_ref[...].reshape(DSA_HEADS * tq, LANES), kb_ref[pl.ds(off, tk), :])
        s = (s.reshape(DSA_HEADS, tq, tk) + bias[None]).reshape(DSA_HEADS * tq, tk)
        m_old = m_scr[...]
        m_new = jnp.maximum(m_old, jnp.max(s, axis=1, keepdims=True))
        p = jnp.exp2(s - jnp.concatenate([m_new] * (tk // LANES), axis=1))
        pv = jnp.dot(p.astype(BF16), va_ref[pl.ds(off, tk), :], preferred_element_type=F32)
        alpha = jnp.exp2(m_old - m_new)
        acc_scr[...] = jnp.concatenate([alpha, alpha], axis=1) * acc_scr[...] + pv
        m_scr[...] = m_new
        return carry

    lax.fori_loop(0, nvis, attn_body, 0)
    for h in range(DSA_HEADS):
        rs = slice(h * tq, (h + 1) * tq)
        o_ref[:, h * LANES:(h + 1) * LANES] = (acc_scr[rs, 0:LANES] / acc_scr[rs, LANES:2 * LANES]).astype(o_ref.dtype)


def _dsa(qi, qb, proj, kia, kib, kb, va):
    S = kb.shape[0]
    tq, tk = min(DSA_TQ, S), DSA_TK
    topk = min(DSA_TOPK, S // 4)
    nq = DSA_HEADS * DSA_HEAD_DIM
    full = lambda w: pl.BlockSpec((S, w), lambda i: (0, 0), pipeline_mode=pl.Buffered(1))
    return pl.pallas_call(
        functools.partial(_dsa_kernel, topk),
        out_shape=jax.ShapeDtypeStruct((S, nq), BF16),
        grid=(S // tq,),
        in_specs=[pl.BlockSpec((IDX_HEADS // 2, tq, LANES), lambda i: (0, i, 0)),
                  pl.BlockSpec((DSA_HEADS, tq, LANES), lambda i: (0, i, 0)),
                  pl.BlockSpec((tq, LANES), lambda i: (i, COL_TAIL // LANES)),
                  full(LANES), full(LANES), full(LANES), full(2 * LANES)],
        out_specs=pl.BlockSpec((tq, nq), lambda i: (i, 0)),
        scratch_shapes=[pltpu.VMEM((S // tk, tq, tk), F32),
                        pltpu.VMEM((IDX_HEADS, tq, LANES), F32),
                        pltpu.VMEM((tq, LANES), F32),
                        pltpu.VMEM((tq, LANES), F32),
                        pltpu.VMEM((DSA_HEADS * tq, LANES), F32),
                        pltpu.VMEM((DSA_HEADS * tq, 2 * LANES), F32)],
        compiler_params=_cparams(("arbitrary",)),
        name="dsa",
    )(qi, qb, proj, kia, kib, kb, va)


def _xattn_kernel(q_ref, kv_ref, o_ref):
    d = XATTN_HEAD_DIM
    nkv = XATTN_HEADS * d
    for h in range(XATTN_HEADS):
        q = q_ref[:, h * d:(h + 1) * d]
        k = kv_ref[:, h * d:(h + 1) * d]
        v = kv_ref[:, nkv + h * d:nkv + (h + 1) * d]
        s = _dot_nt(q, k) * (d ** -0.5)
        s = s - jnp.max(s, axis=1, keepdims=True)
        p = jnp.exp(s)
        p = p / jnp.sum(p, axis=1, keepdims=True)
        o_ref[:, h * d:(h + 1) * d] = jnp.dot(p.astype(BF16), v, preferred_element_type=F32).astype(o_ref.dtype)


def _xattn(q, kv, tm):
    S, N = q.shape
    M = kv.shape[0]
    tm = min(tm, S)
    return pl.pallas_call(
        _xattn_kernel,
        out_shape=jax.ShapeDtypeStruct((S, N), BF16),
        grid=(S // tm,),
        in_specs=[pl.BlockSpec((tm, N), lambda i: (i, 0)),
                  pl.BlockSpec((M, 2 * N), lambda i: (0, 0))],
        out_specs=pl.BlockSpec((tm, N), lambda i: (i, 0)),
        compiler_params=_cparams(("arbitrary",)),
        name="xattn",
    )(q, kv)


PEER_TS = 256


def _top16_distinct(s):
    vals, cnts = [], []
    rem = s
    for _ in range(PEER_TOPK):
        m = jnp.max(rem, axis=0, keepdims=True)
        eq = rem == m
        cnts.append(jnp.sum(jnp.where(eq, 1.0, 0.0), axis=0, keepdims=True))
        vals.append(m)
        rem = jnp.where(eq, -jnp.inf, rem)
    return vals, cnts


def _stack_rows(rows):
    n, T = len(rows), rows[0].shape[1]
    idx = lax.broadcasted_iota(jnp.int32, (n, T), 0)
    out = jnp.broadcast_to(rows[0], (n, T))
    for r in range(1, n):
        out = jnp.where(idx == r, rows[r], out)
    return out


def _peer_select_kernel(q_ref, sk_ref, c_ref, s2_ref, e1_ref, e2_ref):
    T = q_ref.shape[0]
    for h in range(PEER_HEADS):
        s1 = _dot_nt(sk_ref[h, 0], q_ref[:, (2 * h) * PEER_HALF:(2 * h + 1) * PEER_HALF])
        s2 = _dot_nt(sk_ref[h, 1], q_ref[:, (2 * h + 1) * PEER_HALF:(2 * h + 2) * PEER_HALF])
        a_vals, a_cnts = _top16_distinct(s1)
        b_vals, b_cnts = _top16_distinct(s2)
        a_mat, ac_mat = _stack_rows(a_vals), _stack_rows(a_cnts)
        b_mat, bc_mat = _stack_rows(b_vals), _stack_rows(b_cnts)
        k = PEER_TOPK
        row8 = lax.broadcasted_iota(jnp.int32, (8, T), 0)
        row16 = lax.broadcasted_iota(jnp.int32, (k, T), 0)
        ninf = -jnp.inf
        pieces = [
            (a_vals[0] + b_mat, a_cnts[0] * bc_mat),
            (a_vals[1] + b_mat[0:8], a_cnts[1] * bc_mat[0:8]),
            (jnp.where(row8 < k // 3, a_vals[2] + b_mat[0:8], ninf), a_cnts[2] * bc_mat[0:8]),
            (jnp.where(row8 < k // 4, a_vals[3] + b_mat[0:8], ninf), a_cnts[3] * bc_mat[0:8]),
            (jnp.where(row16 >= 4, b_vals[0] + a_mat, ninf), b_cnts[0] * ac_mat),
            (jnp.where(row8 >= 4, b_vals[1] + a_mat[0:8], ninf), b_cnts[1] * ac_mat[0:8]),
            (jnp.where(row8 == 4, b_vals[2] + a_mat[0:8], ninf), b_cnts[2] * ac_mat[0:8]),
        ]
        cand = jnp.concatenate([p[0] for p in pieces], axis=0)
        wgt = jnp.concatenate([p[1] for p in pieces], axis=0)
        rem = cand
        n = jnp.zeros((1, T), F32)
        thr = jnp.full((1, T), -jnp.inf, F32)
        for _ in range(PEER_TOPK):
            m = jnp.max(rem, axis=0, keepdims=True)
            eq = rem == m
            thr = jnp.where(n < PEER_TOPK, m, thr)
            n = n + jnp.sum(jnp.where(eq, wgt, 0.0), axis=0, keepdims=True)
            rem = jnp.where(eq, -jnp.inf, rem)
        vmax = a_vals[0] + b_vals[0]
        keep = cand >= thr
        z = jnp.sum(jnp.where(keep, wgt * jnp.exp(cand - vmax), 0.0), axis=0, keepdims=True)
        inf = jnp.inf
        c_rows = [jnp.min(jnp.where(keep[0:16], b_mat, inf), axis=0, keepdims=True),
                  jnp.min(jnp.where(keep[16:24], b_mat[0:8], inf), axis=0, keepdims=True),
                  jnp.min(jnp.where(keep[24:32], b_mat[0:8], inf), axis=0, keepdims=True),
                  jnp.min(jnp.where(keep[32:40], b_mat[0:8], inf), axis=0, keepdims=True)]
        c_hi = jnp.where(keep[40:56], b_vals[0], inf)
        c_hi = jnp.minimum(c_hi, jnp.concatenate(
            [jnp.minimum(jnp.where(keep[56:64], b_vals[1], inf), jnp.where(keep[64:72], b_vals[2], inf)),
             jnp.full((8, T), inf, F32)], axis=0))
        c = jnp.full(s1.shape, inf, F32)
        for r in range(PEER_TOPK):
            c_r = c_rows[r] if r < 4 else c_hi[r:r + 1]
            c = jnp.where(s1 == a_vals[r], c_r, c)
        c_ref[h] = c
        s2_ref[h] = s2
        e1_ref[h] = jnp.exp(s1 - a_vals[0])
        e2_ref[h] = jnp.exp(s2 - b_vals[0]) / z


def _peer_select(q, sub_keys):
    S = q.shape[0]
    ts = min(PEER_TS, S)
    big = jax.ShapeDtypeStruct((PEER_HEADS, PEER_NKEYS, S), F32)
    bspec = pl.BlockSpec((PEER_HEADS, PEER_NKEYS, ts), lambda i: (0, 0, i))
    return pl.pallas_call(
        _peer_select_kernel,
        out_shape=(big, big, big, big),
        grid=(S // ts,),
        in_specs=[pl.BlockSpec((ts, q.shape[1]), lambda i: (i, 0)),
                  pl.BlockSpec(sub_keys.shape, lambda i: (0, 0, 0, 0))],
        out_specs=(bspec, bspec, bspec, bspec),
        compiler_params=_cparams(("arbitrary",)),
        name="peer_select",
    )(q, sub_keys)


PEER_TM = 512
PEER_TE = 1024


def _gelu_tanh_scaled(x, g):
    inner = x * (0.7978845608028654 + 0.035677408136300125 * (x * x))
    return (g * (0.5 * x)) * (1.0 + jnp.tanh(inner))


def _zero_after(x):
    bits = lax.bitcast_convert_type(x, jnp.uint32)
    z = lax.shift_right_logical(lax.shift_right_logical(bits, jnp.uint32(16)), jnp.uint32(16))
    return z[0:1, :].astype(F32)


def _peer_dense_kernel(xn_ref, u_ref, v_ref, c_ref, s2_ref, e1_ref, e2_ref, o_ref, wt_scr):
    e = pl.program_id(1)
    te, tm = u_ref.shape[0], xn_ref.shape[0]
    nk = PEER_NKEYS
    nr = te // nk
    n_part = 4
    half = te // n_part
    tok_chunk = 2 * LANES
    order = None
    for tc in range(tm // tok_chunk):
        ts = slice(tc * tok_chunk, (tc + 1) * tok_chunk)
        for hf in range(n_part):
            act = _dot_nt(u_ref[hf * half:(hf + 1) * half, :], xn_ref[ts, :])
            for rr in range(nr // n_part):
                r = hf * (nr // n_part) + rr
                c_rows = [c_ref[h, pl.ds(e * nr + r, 1), :] for h in range(PEER_HEADS)]
                e1_rows = [e1_ref[h, pl.ds(e * nr + r, 1), :] for h in range(PEER_HEADS)]
                for tb in range(tok_chunk // LANES):
                    ls = slice(tc * tok_chunk + tb * LANES, tc * tok_chunk + (tb + 1) * LANES)
                    g = jnp.zeros((nk, LANES), F32)
                    for h in range(PEER_HEADS):
                        c_row = c_rows[h][:, ls]
                        if order is not None:
                            c_row = c_row + order
                        g = g + jnp.where(s2_ref[h, :, ls] >= c_row, e2_ref[h, :, ls], 0.0) * e1_rows[h][:, ls]
                    w = _gelu_tanh_scaled(act[rr * nk:(rr + 1) * nk, tb * LANES:(tb + 1) * LANES], g)
                    wt_scr[ls, r * nk:(r + 1) * nk] = w.T.astype(BF16)
            order = _zero_after(w[0:8, :])
        contrib = jnp.dot(wt_scr[ts, :], v_ref[...], preferred_element_type=F32)
        prev = jnp.where(e == 0, 0.0, o_ref[ts, :])
        o_ref[ts, :] = prev + contrib


def _peer_dense(xn, u, v, c, s2, e1, e2):
    S, D = xn.shape
    E = u.shape[0]
    tm, te = min(PEER_TM, S), PEER_TE
    fac = pl.BlockSpec((PEER_HEADS, PEER_NKEYS, tm), lambda i, e: (0, 0, i))
    return pl.pallas_call(
        _peer_dense_kernel,
        out_shape=jax.ShapeDtypeStruct((S, D), F32),
        grid=(S // tm, E // te),
        in_specs=[pl.BlockSpec((tm, D), lambda i, e: (i, 0)),
                  pl.BlockSpec((te, D), lambda i, e: (e, 0)),
                  pl.BlockSpec((te, D), lambda i, e: (e, 0)),
                  fac, fac, fac, fac],
        out_specs=pl.BlockSpec((tm, D), lambda i, e: (i, 0)),
        scratch_shapes=[pltpu.VMEM((tm, te), BF16)],
        compiler_params=_cparams(("arbitrary", "arbitrary")),
        name="peer_dense",
    )(xn, u, v, c, s2, e1, e2)


def _final_kernel(h_ref, y_ref, g_ref, o_ref):
    x = h_ref[...] + y_ref[...]
    o_ref[...] = x * lax.rsqrt(jnp.mean(x * x, axis=-1, keepdims=True) + RMS_EPS) * g_ref[...]


def _final(h, y, g, tm):
    S, D = h.shape
    tm = min(tm, S)
    spec = pl.BlockSpec((tm, D), lambda i: (i, 0))
    return pl.pallas_call(
        _final_kernel,
        out_shape=jax.ShapeDtypeStruct((S, D), F32),
        grid=(S // tm,),
        in_specs=[spec, spec, pl.BlockSpec((1, D), lambda i: (0, 0))],
        out_specs=spec,
        compiler_params=_cparams(("arbitrary",)),
        name="final",
    )(h, y, g.reshape(1, D))


def _permute_w_in(w_in, l):
    c = _SRC
    sl = lambda a, b: w_in[l, :, c[a]:c[b]]
    pad = jnp.zeros((w_in.shape[1], NP_COLS - COL_TAIL - 96), w_in.dtype)
    cols = [sl("gv", "gr"), sl("gr", "glow"), sl("dq", "dk"), sl("ga", "gb"), sl("gb", "end"),
            sl("gq", "gk"), sl("gk", "gv"), sl("iq", "ik"), sl("dk", "dv"), sl("dv", "iq"),
            sl("ik", "iw"), sl("glow", "dq"), sl("iw", "ga"), pad]
    return jnp.concatenate(cols, axis=1).astype(BF16)


def _layer(h, mem, positions, norm_mix_g, w_in_perm, gla_gate_w2, gla_gate_b, gla_head_norm_g,
           w_proj_gla, w_proj_dsa, w_out, norm_x_g, norm_mem_g, w_xq, w_xk, w_xv, w_xo,
           norm_ffn_g, peer_wq, peer_sub_keys, peer_u, peer_v):
    (proj,) = _norm_mm(h, norm_mix_g, w_in_perm, F32, 1024, 1536, "in_proj")
    w2p = jnp.zeros((LANES, GLA_HEADS * GLA_DK), F32).at[TAIL_GLOW:TAIL_GLOW + GLA_GATE_RANK].set(gla_gate_w2)
    o_a = _gla(proj, w2p, gla_gate_b, gla_head_norm_g)
    qb, kb, vb, qi, kia, kib = _rope(proj, positions, 512)
    o_b = _dsa(qi, qb, proj, kia, kib, kb, vb)
    merged = _merge(o_a, o_b, w_proj_gla.astype(BF16), w_proj_dsa.astype(BF16), proj, 512, 2048)
    h = _mm_res(merged, w_out.astype(BF16), h, 512, 2048, "out_proj")

    (q_x,) = _norm_mm(h, norm_x_g, w_xq.astype(BF16), BF16, 512, 2048, "xattn_q")
    (kv_x,) = _norm_mm(mem, norm_mem_g, jnp.concatenate([w_xk, w_xv], axis=1).astype(BF16), BF16, 1024, 512,
                       "xattn_kv")
    o_x = _xattn(q_x, kv_x, 512)
    h = _mm_res(o_x, w_xo.astype(BF16), h, 512, 2048, "xattn_o")

    q_p, xn = _norm_mm(h, norm_ffn_g, peer_wq.astype(BF16), BF16, 512, 2048, "peer_q", keep_xn=True)
    c, s2, e1, e2 = _peer_select(q_p, peer_sub_keys.astype(BF16))
    y = _peer_dense(xn, peer_u.astype(BF16), peer_v.astype(BF16), c, s2, e1, e2)
    return h, y


def _take(a, i):
    return a.reshape(a.shape[1:]) if a.shape[0] == 1 else a[i]


def kernel(x, mem, positions, norm_mix_g, w_in, gla_gate_w2, gla_gate_b, gla_head_norm_g, w_proj_gla, w_proj_dsa, w_out, norm_x_g, norm_mem_g, w_xq, w_xk, w_xv, w_xo, norm_ffn_g, peer_wq, peer_sub_keys, peer_u, peer_v, norm_final_g):
    B, S, D = x.shape
    depth = w_in.shape[0]
    outs = []
    for b in range(B):
        h = _take(x, b)
        y = None
        for l in range(depth):
            if y is not None:
                h = h + y
            h, y = _layer(h, _take(mem, b), _take(positions, b), _take(norm_mix_g, l), _permute_w_in(w_in, l),
                          _take(gla_gate_w2, l), _take(gla_gate_b, l), _take(gla_head_norm_g, l),
                          _take(w_proj_gla, l), _take(w_proj_dsa, l), _take(w_out, l), _take(norm_x_g, l),
                          _take(norm_mem_g, l), _take(w_xq, l), _take(w_xk, l), _take(w_xv, l), _take(w_xo, l),
                          _take(norm_ffn_g, l), _take(peer_wq, l), _take(peer_sub_keys, l), _take(peer_u, l),
                          _take(peer_v, l))
        outs.append(_final(h, y, norm_final_g, 512))
    return jnp.stack(outs)
```
